```python
import functools
import jax, jax.numpy as jnp
from jax import lax
import numpy as np

D_MODEL = 2048
BATCH = 8
SEQ = 2048
DEPTH = 2

MIX_WIDTH = D_MODEL
HEAD_DIM = 128
Q_BLOCK = 128
SB_WIDTH = MIX_WIDTH // 2
SB_HEADS = SB_WIDTH // HEAD_DIM
SC_WIDTH = MIX_WIDTH - SB_WIDTH
SC_GROUPS = SC_WIDTH // HEAD_DIM
CONV_WIDTH = 3
CHUNK = 128
SG_WIDTH = MIX_WIDTH // 2
SG_GROUP_DIM = 128
SG_GROUPS = SG_WIDTH // SG_GROUP_DIM
FOX_WIDTH = MIX_WIDTH - SG_WIDTH
FOX_HEADS = FOX_WIDTH // HEAD_DIM
IN_AB = 3 * SB_WIDTH + 3 * SC_WIDTH
IN_CD = 2 * SG_WIDTH + 3 * FOX_WIDTH + FOX_HEADS
D_FF = 5632
EPS = 1e-6

kernel_name = "hybrid_stickbreak_shortconv_chunkgmlp_fox_block"


def rmsnorm(x, g):
    xf = x.astype(jnp.float32)
    y = xf * lax.rsqrt(jnp.mean(xf * xf, axis=-1, keepdims=True) + EPS)
    return (y * g.astype(jnp.float32)).astype(x.dtype)


def layernorm(x, g):
    xf = x.astype(jnp.float32)
    mu = jnp.mean(xf, axis=-1, keepdims=True)
    xc = xf - mu
    y = xc * lax.rsqrt(jnp.mean(xc * xc, axis=-1, keepdims=True) + EPS)
    return (y * g.astype(jnp.float32)).astype(x.dtype)


def causal_dwconv(x, w):
    K = w.shape[0]
    S = x.shape[1]
    xp = jnp.pad(x, ((0, 0), (K - 1, 0), (0, 0)))
    y = xp[:, 0:S] * w[0]
    for j in range(1, K):
        y = y + xp[:, j:j + S] * w[j]
    return y


def split_heads(t, n_heads):
    return t.reshape(t.shape[0], t.shape[1], n_heads, -1)


def stick_breaking_attention(q, k, v):
    S = q.shape[1]
    scale = HEAD_DIM ** -0.5
    outs = []
    for i in range(S // Q_BLOCK):
        q0 = i * Q_BLOCK
        kend = q0 + Q_BLOCK
        qb = q[:, q0:kend].astype(jnp.float32)
        kb = k[:, :kend].astype(jnp.float32)
        vb = v[:, :kend].astype(jnp.float32)
        z = jnp.einsum('bqhd,bkhd->bhqk', qb, kb) * scale
        t_idx = q0 + jnp.arange(Q_BLOCK)[:, None]
        s_idx = jnp.arange(kend)[None, :]
        mask = s_idx < t_idx
        log_1mb = jnp.where(mask, jax.nn.log_sigmoid(-z), 0.0)
        later = lax.cumsum(log_1mb, axis=3, reverse=True) - log_1mb
        a = jnp.where(mask, jnp.exp(jax.nn.log_sigmoid(z) + later), 0.0)
        outs.append(jnp.einsum('bhqk,bkhd->bqhd', a, vb))
    return jnp.concatenate(outs, axis=1).astype(q.dtype)


def forgetting_attention(q, k, v, log_f):
    S = q.shape[1]
    scale = HEAD_DIM ** -0.5
    c = jnp.cumsum(log_f, axis=1).transpose(0, 2, 1)
    outs = []
    for i in range(S // Q_BLOCK):
        q0 = i * Q_BLOCK
        kend = q0 + Q_BLOCK
        qb = q[:, q0:kend].astype(jnp.float32)
        kb = k[:, :kend].astype(jnp.float32)
        vb = v[:, :kend].astype(jnp.float32)
        logits = jnp.einsum('bqhd,bkhd->bhqk', qb, kb) * scale
        logits = logits + c[:, :, q0:kend, None] - c[:, :, None, :kend]
        t_idx = q0 + jnp.arange(Q_BLOCK)[:, None]
        s_idx = jnp.arange(kend)[None, :]
        p = jax.nn.softmax(jnp.where(s_idx <= t_idx, logits, -jnp.inf), axis=-1)
        outs.append(jnp.einsum('bhqk,bkhd->bqhd', p, vb))
    return jnp.concatenate(outs, axis=1).astype(q.dtype)


def chunked_spatial_gate(u, v, w_s, b_s, g):
    B, S, W = v.shape
    v = layernorm(v, g)
    vc = v.reshape(B, S // CHUNK, CHUNK, SG_GROUPS, SG_GROUP_DIM)
    w = w_s * jnp.tril(jnp.ones((CHUNK, CHUNK), w_s.dtype))
    mixed = jnp.einsum('gts,bnsgc->bntgc', w, vc) + b_s.T[None, None, :, :, None]
    return u * mixed.reshape(B, S, W)


def mixer_ab(h, w_in, sc_conv_w, w_out):
    B, S, _ = h.shape
    p = h @ w_in
    q, k, v, gate_b, gate_c, hin = jnp.split(
        p, [SB_WIDTH, 2 * SB_WIDTH, 3 * SB_WIDTH,
            3 * SB_WIDTH + SC_WIDTH, 3 * SB_WIDTH + 2 * SC_WIDTH], axis=-1)
    a_out = stick_breaking_attention(split_heads(q, SB_HEADS), split_heads(k, SB_HEADS),
                                     split_heads(v, SB_HEADS)).reshape(B, S, SB_WIDTH)
    b_out = gate_b * causal_dwconv(gate_c * hin, sc_conv_w)
    return jnp.concatenate([a_out, b_out], axis=-1) @ w_out


def mixer_cd(h, w_in, fox_b_f, sg_w, sg_b, sg_norm_g, w_out):
    B, S, _ = h.shape
    p = h @ w_in
    u, v, q, k, vv, f = jnp.split(
        p, [SG_WIDTH, 2 * SG_WIDTH, 2 * SG_WIDTH + FOX_WIDTH,
            2 * SG_WIDTH + 2 * FOX_WIDTH, 2 * SG_WIDTH + 3 * FOX_WIDTH], axis=-1)
    c_out = chunked_spatial_gate(jax.nn.gelu(u), jax.nn.gelu(v), sg_w, sg_b, sg_norm_g)
    log_f = jax.nn.log_sigmoid(f.astype(jnp.float32) + fox_b_f.astype(jnp.float32))
    d_out = forgetting_attention(split_heads(q, FOX_HEADS), split_heads(k, FOX_HEADS),
                                 split_heads(vv, FOX_HEADS), log_f).reshape(B, S, FOX_WIDTH)
    return jnp.concatenate([c_out, d_out], axis=-1) @ w_out


def conv_ffn(h, w_up, conv_w, w_down):
    a = causal_dwconv(h @ w_up, conv_w)
    gate, up = jnp.split(a, 2, axis=-1)
    return (jax.nn.silu(gate) * up) @ w_down


def setup_inputs(seed: int = 0) -> dict:
    key = jax.random.key(seed)
    ks = iter(jax.random.split(key, 32))
    f32 = jnp.float32

    def w(shape, fan_in):
        return jax.random.normal(next(ks), shape, f32) * (fan_in ** -0.5)

    def gain(n):
        return 1.0 + 0.02 * jax.random.normal(next(ks), (n,), f32)

    inp = {}
    inp["x"] = jax.random.normal(next(ks), (BATCH, SEQ, D_MODEL), f32)
    inp["l0_mix_norm_g"] = gain(D_MODEL)
    inp["l0_w_in"] = w((D_MODEL, IN_AB), D_MODEL)
    inp["l0_sc_conv_w"] = w((CONV_WIDTH, SC_WIDTH), CONV_WIDTH)
    inp["l0_w_out"] = w((MIX_WIDTH, D_MODEL), MIX_WIDTH)
    inp["l0_ffn_norm_g"] = gain(D_MODEL)
    inp["l0_ffn_up"] = w((D_MODEL, 2 * D_FF), D_MODEL)
    inp["l0_ffn_conv_w"] = w((CONV_WIDTH, 2 * D_FF), CONV_WIDTH)
    inp["l0_ffn_down"] = w((D_FF, D_MODEL), D_FF)
    inp["l1_mix_norm_g"] = gain(D_MODEL)
    inp["l1_w_in"] = w((D_MODEL, IN_CD), D_MODEL)
    inp["l1_fox_b_f"] = 1.0 + 0.5 * jax.random.normal(next(ks), (FOX_HEADS,), f32)
    inp["l1_sg_w"] = w((SG_GROUPS, CHUNK, CHUNK), CHUNK)
    inp["l1_sg_b"] = 1.0 + 0.1 * jax.random.normal(next(ks), (SG_GROUPS, CHUNK), f32)
    inp["l1_sg_norm_g"] = gain(SG_WIDTH)
    inp["l1_w_out"] = w((MIX_WIDTH, D_MODEL), MIX_WIDTH)
    inp["l1_ffn_norm_g"] = gain(D_MODEL)
    inp["l1_ffn_up"] = w((D_MODEL, 2 * D_FF), D_MODEL)
    inp["l1_ffn_conv_w"] = w((CONV_WIDTH, 2 * D_FF), CONV_WIDTH)
    inp["l1_ffn_down"] = w((D_FF, D_MODEL), D_FF)
    inp["final_norm_g"] = gain(D_MODEL)
    return inp


def reference(x, l0_mix_norm_g, l0_w_in, l0_sc_conv_w, l0_w_out, l0_ffn_norm_g,
              l0_ffn_up, l0_ffn_conv_w, l0_ffn_down,
              l1_mix_norm_g, l1_w_in, l1_fox_b_f, l1_sg_w, l1_sg_b, l1_sg_norm_g,
              l1_w_out, l1_ffn_norm_g, l1_ffn_up, l1_ffn_conv_w, l1_ffn_down,
              final_norm_g):
    layers = [
        (l0_mix_norm_g,
         functools.partial(mixer_ab, w_in=l0_w_in, sc_conv_w=l0_sc_conv_w, w_out=l0_w_out),
         l0_ffn_norm_g, l0_ffn_up, l0_ffn_conv_w, l0_ffn_down),
        (l1_mix_norm_g,
         functools.partial(mixer_cd, w_in=l1_w_in, fox_b_f=l1_fox_b_f, sg_w=l1_sg_w,
                           sg_b=l1_sg_b, sg_norm_g=l1_sg_norm_g, w_out=l1_w_out),
         l1_ffn_norm_g, l1_ffn_up, l1_ffn_conv_w, l1_ffn_down),
    ]
    for i in range(DEPTH):
        mix_g, mixer, ffn_g, w_up, conv_w, w_down = layers[i]
        x = x + mixer(rmsnorm(x, mix_g))
        x = x + conv_ffn(rmsnorm(x, ffn_g), w_up, conv_w, w_down)
    return rmsnorm(x, final_norm_g)
```

```python
import functools

import jax
import jax.numpy as jnp
from jax import lax
from jax.experimental import pallas as pl
from jax.experimental.pallas import tpu as pltpu

EPS = 1e-6
HEAD_DIM = 128
CHUNK = 128
SG_GROUP_DIM = 128
ATT_BLOCK = 256

V7X_VMEM_BYTES = 64 * 1024 * 1024
MIB = 1024 * 1024

F32 = jnp.float32
BF16 = jnp.bfloat16


def _params(semantics, vmem_mib):
    assert vmem_mib * MIB < V7X_VMEM_BYTES
    return pltpu.CompilerParams(dimension_semantics=semantics, vmem_limit_bytes=vmem_mib * MIB)


def _tile(n, target, quantum=128):
    if n <= target:
        return n
    t = (target // quantum) * quantum
    while n % t:
        t -= quantum
    return t


def _dot(a, b):
    return jnp.dot(a, b, preferred_element_type=F32)


def _dot_nt(a, b):
    return lax.dot_general(a, b, (((1,), (1,)), ((), ())), preferred_element_type=F32)


def _softplus(z):
    return jnp.maximum(z, 0.0) + jnp.log1p(jnp.exp(-jnp.abs(z)))


def _gelu_tanh(x):
    return 0.5 * x * (1.0 + jnp.tanh(0.7978845608028654 * (x + 0.044715 * (x * x * x))))


def _shift_rows(a, k):
    rolled = pltpu.roll(a, k, 0)
    row = lax.broadcasted_iota(jnp.int32, a.shape, 0)
    return jnp.where(row >= k, rolled, 0.0)


def _causal_conv3(a, cw):
    return _shift_rows(a, 2) * cw[0:1, :] + _shift_rows(a, 1) * cw[1:2, :] + a * cw[2:3, :]


def _rmsnorm_body(x_ref, g_ref, o_ref):
    x = x_ref[...]
    ms = jnp.mean(x * x, axis=-1, keepdims=True)
    o_ref[...] = (x * lax.rsqrt(ms + EPS) * g_ref[...]).astype(o_ref.dtype)


def rmsnorm(x, g, out_dtype):
    T, D = x.shape
    tm = min(T, 512)
    return pl.pallas_call(
        _rmsnorm_body,
        grid=(T // tm,),
        in_specs=[pl.BlockSpec((tm, D), lambda i: (i, 0)), pl.BlockSpec((1, D), lambda i: (0, 0))],
        out_specs=pl.BlockSpec((tm, D), lambda i: (i, 0)),
        out_shape=jax.ShapeDtypeStruct((T, D), out_dtype),
        compiler_params=_params(("parallel",), 40),
        name="rmsnorm",
    )(x, g.reshape(1, D).astype(F32))


def _mm_body(a_ref, w_ref, o_ref):
    o_ref[...] = _dot(a_ref[...], w_ref[...]).astype(o_ref.dtype)


def _mm_res_body(a_ref, w_ref, r_ref, o_ref):
    o_ref[...] = r_ref[...] + _dot(a_ref[...], w_ref[...])


def matmul(a, w, res=None, *, tm, tn, out_dtype, name):
    M, K = a.shape
    N = w.shape[1]
    tm, tn = _tile(M, tm), _tile(N, tn)
    in_specs = [pl.BlockSpec((tm, K), lambda i, j: (i, 0)), pl.BlockSpec((K, tn), lambda i, j: (0, j))]
    args = [a, w]
    body = _mm_body
    if res is not None:
        in_specs.append(pl.BlockSpec((tm, tn), lambda i, j: (i, j)))
        args.append(res)
        body = _mm_res_body
    return pl.pallas_call(
        body,
        grid=(M // tm, N // tn),
        in_specs=in_specs,
        out_specs=pl.BlockSpec((tm, tn), lambda i, j: (i, j)),
        out_shape=jax.ShapeDtypeStruct((M, N), out_dtype),
        compiler_params=_params(("parallel", "arbitrary"), 56),
        name=name,
    )(*args)


def _outproj_body(a1_ref, a2_ref, w1_ref, w2_ref, r_ref, g_ref, x_ref, h_ref):
    x = r_ref[...] + _dot(a1_ref[...], w1_ref[...]) + _dot(a2_ref[...], w2_ref[...])
    x_ref[...] = x
    ms = jnp.mean(x * x, axis=-1, keepdims=True)
    h_ref[...] = (x * lax.rsqrt(ms + EPS) * g_ref[...]).astype(h_ref.dtype)


def out_projection(a1, a2, w, res, g):
    T, K1 = a1.shape
    K2 = a2.shape[1]
    D = w.shape[1]
    assert K1 == K2 and w.shape[0] == K1 + K2
    tm = min(T, 512)
    return pl.pallas_call(
        _outproj_body,
        grid=(T // tm,),
        in_specs=[
            pl.BlockSpec((tm, K1), lambda i: (i, 0)),
            pl.BlockSpec((tm, K2), lambda i: (i, 0)),
            pl.BlockSpec((K1, D), lambda i: (0, 0)),
            pl.BlockSpec((K2, D), lambda i: (1, 0)),
            pl.BlockSpec((tm, D), lambda i: (i, 0)),
            pl.BlockSpec((1, D), lambda i: (0, 0)),
        ],
        out_specs=[pl.BlockSpec((tm, D), lambda i: (i, 0)), pl.BlockSpec((tm, D), lambda i: (i, 0))],
        out_shape=[jax.ShapeDtypeStruct((T, D), F32), jax.ShapeDtypeStruct((T, D), BF16)],
        compiler_params=_params(("parallel",), 56),
        name="out_projection",
    )(a1, a2, w, w, res, g.reshape(1, D).astype(F32))


def _ffn_up_body(h_ref, wg_ref, wu_ref, cg_ref, cu_ref, o_ref):
    h = h_ref[...]
    gate = _causal_conv3(_dot(h, wg_ref[...]), cg_ref[...])
    up = _causal_conv3(_dot(h, wu_ref[...]), cu_ref[...])
    o_ref[...] = (gate * (1.0 / (1.0 + jnp.exp(-gate))) * up).astype(o_ref.dtype)


def ffn_up(h, w_up, conv_w, *, seq, tn):
    T, D = h.shape
    F = w_up.shape[1] // 2
    tn = _tile(F, tn)
    assert T % seq == 0
    nf = F // tn
    return pl.pallas_call(
        _ffn_up_body,
        grid=(T // seq, nf),
        in_specs=[
            pl.BlockSpec((seq, D), lambda b, j: (b, 0)),
            pl.BlockSpec((D, tn), lambda b, j: (0, j)),
            pl.BlockSpec((D, tn), lambda b, j: (0, nf + j)),
            pl.BlockSpec((3, tn), lambda b, j: (0, j)),
            pl.BlockSpec((3, tn), lambda b, j: (0, nf + j)),
        ],
        out_specs=pl.BlockSpec((seq, tn), lambda b, j: (b, j)),
        out_shape=jax.ShapeDtypeStruct((T, F), BF16),
        compiler_params=_params(("parallel", "arbitrary"), 56),
        name="ffn_up",
    )(h, w_up, w_up, conv_w, conv_w)


def _short_conv_body(gb_ref, gc_ref, hin_ref, cw_ref, o_ref):
    x = gc_ref[...].astype(F32) * hin_ref[...].astype(F32)
    o_ref[...] = (gb_ref[...].astype(F32) * _causal_conv3(x, cw_ref[...])).astype(o_ref.dtype)


def gated_short_conv(p, conv_w, *, seq, col0, width):
    T = p.shape[0]
    tc = min(width, 256)
    assert col0 % tc == 0 and width % tc == 0
    c0, nw = col0 // tc, width // tc
    return pl.pallas_call(
        _short_conv_body,
        grid=(T // seq, nw),
        in_specs=[
            pl.BlockSpec((seq, tc), lambda b, j: (b, c0 + j)),
            pl.BlockSpec((seq, tc), lambda b, j: (b, c0 + nw + j)),
            pl.BlockSpec((seq, tc), lambda b, j: (b, c0 + 2 * nw + j)),
            pl.BlockSpec((3, tc), lambda b, j: (0, j)),
        ],
        out_specs=pl.BlockSpec((seq, tc), lambda b, j: (b, j)),
        out_shape=jax.ShapeDtypeStruct((T, width), BF16),
        compiler_params=_params(("parallel", "parallel"), 32),
        name="gated_short_conv",
    )(p, p, p, conv_w)


def _sb_attention_body(q_ref, k_ref, v_ref, o_ref, *, n_heads, scale):
    blk = ATT_BLOCK
    i = pl.program_id(1)
    row = lax.broadcasted_iota(jnp.int32, (blk, blk), 0)
    col = lax.broadcasted_iota(jnp.int32, (blk, blk), 1)
    strict = col < row
    suffix = jnp.where(row >= col, 1.0, 0.0).astype(BF16)

    for h in range(n_heads):
        hs = slice(h * HEAD_DIM, (h + 1) * HEAD_DIM)
        q = q_ref[:, hs]

        def block(j, acc, r, diagonal):
            start = pl.multiple_of(j * blk, blk)
            kb = k_ref[pl.ds(start, blk), hs]
            vb = v_ref[pl.ds(start, blk), hs]
            z = _dot_nt(q, kb) * scale
            sp = _softplus(z)
            if diagonal:
                sp = jnp.where(strict, sp, 0.0)
            sp_hi = sp.astype(BF16)
            sp_lo = (sp - sp_hi.astype(F32)).astype(BF16)
            cs = _dot(sp_hi, suffix) + _dot(sp_lo, suffix)
            a = jnp.exp(z - (cs + r))
            if diagonal:
                a = jnp.where(strict, a, 0.0)
            acc = acc + _dot(a.astype(BF16), vb)
            return acc, r + cs[:, 0:1]

        acc, r = block(i, jnp.zeros((blk, HEAD_DIM), F32), jnp.zeros((blk, 1), F32), True)

        def body(n, carry):
            return block(i - 1 - n, carry[0], carry[1], False)

        acc, r = lax.fori_loop(0, i, body, (acc, r))
        o_ref[:, hs] = acc.astype(o_ref.dtype)


def sb_attention(p, *, seq, n_heads):
    T = p.shape[0]
    W = n_heads * HEAD_DIM
    blk = ATT_BLOCK
    assert seq % blk == 0
    nq = seq // blk
    body = functools.partial(_sb_attention_body, n_heads=n_heads, scale=HEAD_DIM ** -0.5)
    return pl.pallas_call(
        body,
        grid=(T // seq, nq),
        in_specs=[
            pl.BlockSpec((blk, W), lambda b, i: (b * nq + i, 0)),
            pl.BlockSpec((seq, W), lambda b, i: (b, 1)),
            pl.BlockSpec((seq, W), lambda b, i: (b, 2)),
        ],
        out_specs=pl.BlockSpec((blk, W), lambda b, i: (b * nq + i, 0)),
        out_shape=jax.ShapeDtypeStruct((T, W), BF16),
        compiler_params=_params(("parallel", "arbitrary"), 48),
        name="sb_attention",
    )(p, p, p)


def _forget_cumsum_body(f_ref, b_ref, c_ref):
    x = f_ref[0] + b_ref[...]
    lf = -_softplus(-x)
    S = lf.shape[1]
    lane = lax.broadcasted_iota(jnp.int32, lf.shape, 1)
    k = 1
    while k < S:
        lf = lf + jnp.where(lane >= k, pltpu.roll(lf, k, 1), 0.0)
        k *= 2
    c_ref[0] = lf


def forget_cumsum(f, bias):
    B, H, S = f.shape
    return pl.pallas_call(
        _forget_cumsum_body,
        grid=(B,),
        in_specs=[pl.BlockSpec((1, H, S), lambda b: (b, 0, 0)), pl.BlockSpec((H, 1), lambda b: (0, 0))],
        out_specs=pl.BlockSpec((1, H, S), lambda b: (b, 0, 0)),
        out_shape=jax.ShapeDtypeStruct((B, H, S), F32),
        compiler_params=_params(("parallel",), 16),
        name="forget_cumsum",
    )(f, bias.reshape(H, 1).astype(F32))


def _fox_attention_body(q_ref, k_ref, v_ref, ccol_ref, crow_ref, o_ref, *, n_heads, scale):
    blk = ATT_BLOCK
    i = pl.program_id(1)
    row = lax.broadcasted_iota(jnp.int32, (blk, blk), 0)
    col = lax.broadcasted_iota(jnp.int32, (blk, blk), 1)
    causal = col <= row

    for h in range(n_heads):
        hs = slice(h * HEAD_DIM, (h + 1) * HEAD_DIM)
        q = q_ref[:, hs]
        c_t = ccol_ref[:, h:h + 1]

        def block(j, m, l, acc, diagonal):
            start = pl.multiple_of(j * blk, blk)
            kb = k_ref[pl.ds(start, blk), hs]
            vb = v_ref[pl.ds(start, blk), hs]
            c_s = crow_ref[0, pl.ds(j, 1), h:h + 1, :][0]
            s = _dot_nt(q, kb) * scale + (c_t - c_s)
            if diagonal:
                s = jnp.where(causal, s, -jnp.inf)
            m_new = jnp.maximum(m, jnp.max(s, axis=-1, keepdims=True))
            alpha = jnp.exp(m - m_new)
            p = jnp.exp(s - m_new)
            l = alpha * l + jnp.sum(p, axis=-1, keepdims=True)
            acc = alpha * acc + _dot(p.astype(BF16), vb)
            return m_new, l, acc

        m, l, acc = block(i, jnp.full((blk, 1), -jnp.inf, F32), jnp.zeros((blk, 1), F32),
                          jnp.zeros((blk, HEAD_DIM), F32), True)

        def body(n, carry):
            return block(i - 1 - n, carry[0], carry[1], carry[2], False)

        m, l, acc = lax.fori_loop(0, i, body, (m, l, acc))
        o_ref[:, hs] = (acc / l).astype(o_ref.dtype)


def fox_attention(p, c, *, seq, n_heads, col_block):
    T = p.shape[0]
    B = T // seq
    W = n_heads * HEAD_DIM
    blk = ATT_BLOCK
    nq = seq // blk
    c_col = c.transpose(0, 2, 1).reshape(T, n_heads)
    c_row = c.reshape(B, n_heads, nq, blk).transpose(0, 2, 1, 3)
    body = functools.partial(_fox_attention_body, n_heads=n_heads, scale=HEAD_DIM ** -0.5)
    return pl.pallas_call(
        body,
        grid=(B, nq),
        in_specs=[
            pl.BlockSpec((blk, W), lambda b, i: (b * nq + i, col_block)),
            pl.BlockSpec((seq, W), lambda b, i: (b, col_block + 1)),
            pl.BlockSpec((seq, W), lambda b, i: (b, col_block + 2)),
            pl.BlockSpec((blk, n_heads), lambda b, i: (b * nq + i, 0)),
            pl.BlockSpec((1, nq, n_heads, blk), lambda b, i: (b, 0, 0, 0)),
        ],
        out_specs=pl.BlockSpec((blk, W), lambda b, i: (b * nq + i, 0)),
        out_shape=jax.ShapeDtypeStruct((T, W), BF16),
        compiler_params=_params(("parallel", "arbitrary"), 48),
        name="fox_attention",
    )(p, p, p, c_col, c_row)


def _spatial_gate_body(u_ref, v_ref, w_ref, bt_ref, g_ref, o_ref, *, n_groups):
    tm = u_ref.shape[0]
    v = _gelu_tanh(v_ref[...].astype(F32))
    mu = jnp.mean(v, axis=-1, keepdims=True)
    vc = v - mu
    var = jnp.mean(vc * vc, axis=-1, keepdims=True)
    vn = (vc * lax.rsqrt(var + EPS) * g_ref[...]).astype(BF16)
    row = lax.broadcasted_iota(jnp.int32, (CHUNK, CHUNK), 0)
    col = lax.broadcasted_iota(jnp.int32, (CHUNK, CHUNK), 1)
    for g in range(n_groups):
        gs = slice(g * SG_GROUP_DIM, (g + 1) * SG_GROUP_DIM)
        w = jnp.where(col <= row, w_ref[g], 0.0).astype(BF16)
        b = bt_ref[:, g:g + 1]
        for c in range(tm // CHUNK):
            cs = slice(c * CHUNK, (c + 1) * CHUNK)
            mixed = _dot(w, vn[cs, gs]) + b
            o_ref[cs, gs] = (_gelu_tanh(u_ref[cs, gs].astype(F32)) * mixed).astype(o_ref.dtype)


def spatial_gate(p, sg_w, sg_b, sg_g, *, width):
    T = p.shape[0]
    G = width // SG_GROUP_DIM
    tm = min(T, 512)
    assert tm % CHUNK == 0
    body = functools.partial(_spatial_gate_body, n_groups=G)
    return pl.pallas_call(
        body,
        grid=(T // tm,),
        in_specs=[
            pl.BlockSpec((tm, width), lambda i: (i, 0)),
            pl.BlockSpec((tm, width), lambda i: (i, 1)),
            pl.BlockSpec((G, CHUNK, CHUNK), lambda i: (0, 0, 0)),
            pl.BlockSpec((CHUNK, G), lambda i: (0, 0)),
            pl.BlockSpec((1, width), lambda i: (0, 0)),
        ],
        out_specs=pl.BlockSpec((tm, width), lambda i: (i, 0)),
        out_shape=jax.ShapeDtypeStruct((T, width), BF16),
        compiler_params=_params(("parallel",), 32),
        name="spatial_gate",
    )(p, p, sg_w, sg_b.T, sg_g.reshape(1, width).astype(F32))


def _conv_ffn(x, h, w_up, conv_w, w_down, seq):
    g = ffn_up(h, w_up.astype(BF16), conv_w, seq=seq, tn=256)
    return matmul(g, w_down.astype(BF16), x, tm=1024, tn=512, out_dtype=F32, name="ffn_down")


def kernel(x, l0_mix_norm_g, l0_w_in, l0_sc_conv_w, l0_w_out, l0_ffn_norm_g, l0_ffn_up, l0_ffn_conv_w, l0_ffn_down, l1_mix_norm_g, l1_w_in, l1_fox_b_f, l1_sg_w, l1_sg_b, l1_sg_norm_g, l1_w_out, l1_ffn_norm_g, l1_ffn_up, l1_ffn_conv_w, l1_ffn_down, final_norm_g):
    B, S, D = x.shape
    T = B * S
    half = D // 2
    n_heads = half // HEAD_DIM
    x = x.reshape(T, D)

    h = rmsnorm(x, l0_mix_norm_g, BF16)
    p = matmul(h, l0_w_in.astype(BF16), tm=1024, tn=1024, out_dtype=BF16, name="in_projection0")
    a_out = sb_attention(p, seq=S, n_heads=n_heads)
    b_out = gated_short_conv(p, l0_sc_conv_w, seq=S, col0=3 * half, width=half)
    x, h = out_projection(a_out, b_out, l0_w_out.astype(BF16), x, l0_ffn_norm_g)
    x = _conv_ffn(x, h, l0_ffn_up, l0_ffn_conv_w, l0_ffn_down, S)

    h = rmsnorm(x, l1_mix_norm_g, BF16)
    n_main = 5 * half
    p = matmul(h, l1_w_in[:, :n_main].astype(BF16), tm=1024, tn=1024, out_dtype=BF16, name="in_projection1")
    w_f = jnp.pad(l1_w_in[:, n_main:], ((0, 0), (0, HEAD_DIM - n_heads))).astype(BF16)
    f = matmul(h, w_f, tm=1024, tn=HEAD_DIM, out_dtype=F32, name="forget_projection")
    f = f[:, :n_heads].reshape(B, S, n_heads).transpose(0, 2, 1)
    c = forget_cumsum(f, l1_fox_b_f)
    c_out = spatial_gate(p, l1_sg_w, l1_sg_b, l1_sg_norm_g, width=half)
    d_out = fox_attention(p, c, seq=S, n_heads=n_heads, col_block=2)
    x, h = out_projection(c_out, d_out, l1_w_out.astype(BF16), x, l1_ffn_norm_g)
    x = _conv_ffn(x, h, l1_ffn_up, l1_ffn_conv_w, l1_ffn_down, S)

    return rmsnorm(x, final_norm_g, F32).reshape(B, S, D)
```

```python
import functools

import jax
import jax.numpy as jnp
from jax import lax
from jax.experimental import pallas as pl
from jax.experimental.pallas import tpu as pltpu

EPS = 1e-6
HEAD_DIM = 128
CHUNK = 128
SG_GROUP_DIM = 128
ATT_BLOCK = 256
DEAD_EXPONENT = 105.0
NORM_SLACK = 1.01

V7X_VMEM_BYTES = 64 * 1024 * 1024
MIB = 1024 * 1024

F32 = jnp.float32
BF16 = jnp.bfloat16


def _params(semantics, vmem_mib):
    assert vmem_mib * MIB < V7X_VMEM_BYTES
    return pltpu.CompilerParams(dimension_semantics=semantics, vmem_limit_bytes=vmem_mib * MIB)


def _tile(n, target, quantum=128):
    if n <= target:
        return n
    t = (target // quantum) * quantum
    while n % t:
        t -= quantum
    return t


def _dot(a, b):
    return jnp.dot(a, b, preferred_element_type=F32)


def _dot_nt(a, b):
    return lax.dot_general(a, b, (((1,), (1,)), ((), ())), preferred_element_type=F32)


def _softplus(z):
    return jnp.maximum(z, 0.0) + jnp.log(1.0 + jnp.exp(-jnp.abs(z)))


def _gelu_tanh(x):
    return 0.5 * x * (1.0 + jnp.tanh(0.7978845608028654 * (x + 0.044715 * (x * x * x))))


def _shift_rows(a, k):
    rolled = pltpu.roll(a, k, 0)
    row = lax.broadcasted_iota(jnp.int32, a.shape, 0)
    return jnp.where(row >= k, rolled, 0.0)


def _causal_conv3(a, cw):
    return _shift_rows(a, 2) * cw[0:1, :] + _shift_rows(a, 1) * cw[1:2, :] + a * cw[2:3, :]


def _rmsnorm_body(x_ref, g_ref, o_ref):
    x = x_ref[...]
    ms = jnp.mean(x * x, axis=-1, keepdims=True)
    o_ref[...] = (x * lax.rsqrt(ms + EPS) * g_ref[...]).astype(o_ref.dtype)


def rmsnorm(x, g, out_dtype):
    T, D = x.shape
    tm = min(T, 512)
    return pl.pallas_call(
        _rmsnorm_body,
        grid=(T // tm,),
        in_specs=[pl.BlockSpec((tm, D), lambda i: (i, 0)), pl.BlockSpec((1, D), lambda i: (0, 0))],
        out_specs=pl.BlockSpec((tm, D), lambda i: (i, 0)),
        out_shape=jax.ShapeDtypeStruct((T, D), out_dtype),
        compiler_params=_params(("parallel",), 40),
        name="rmsnorm",
    )(x, g.reshape(1, D).astype(F32))


def _mm_body(a_ref, w_ref, o_ref):
    o_ref[...] = _dot(a_ref[...], w_ref[...]).astype(o_ref.dtype)


def _mm_res_body(a_ref, w_ref, r_ref, o_ref):
    o_ref[...] = r_ref[...] + _dot(a_ref[...], w_ref[...])


def matmul(a, w, res=None, *, tm, tn, out_dtype, name):
    M, K = a.shape
    N = w.shape[1]
    tm, tn = _tile(M, tm), _tile(N, tn)
    in_specs = [pl.BlockSpec((tm, K), lambda i, j: (i, 0)), pl.BlockSpec((K, tn), lambda i, j: (0, j))]
    args = [a, w]
    body = _mm_body
    if res is not None:
        in_specs.append(pl.BlockSpec((tm, tn), lambda i, j: (i, j)))
        args.append(res)
        body = _mm_res_body
    return pl.pallas_call(
        body,
        grid=(M // tm, N // tn),
        in_specs=in_specs,
        out_specs=pl.BlockSpec((tm, tn), lambda i, j: (i, j)),
        out_shape=jax.ShapeDtypeStruct((M, N), out_dtype),
        compiler_params=_params(("parallel", "arbitrary"), 56),
        name=name,
    )(*args)


def _outproj_body(a1_ref, a2_ref, w1_ref, w2_ref, r_ref, g_ref, x_ref, h_ref):
    x = r_ref[...] + _dot(a1_ref[...], w1_ref[...]) + _dot(a2_ref[...], w2_ref[...])
    x_ref[...] = x
    ms = jnp.mean(x * x, axis=-1, keepdims=True)
    h_ref[...] = (x * lax.rsqrt(ms + EPS) * g_ref[...]).astype(h_ref.dtype)


def out_projection(a1, a2, w, res, g):
    T, K1 = a1.shape
    K2 = a2.shape[1]
    D = w.shape[1]
    assert K1 == K2 and w.shape[0] == K1 + K2
    tm = min(T, 512)
    return pl.pallas_call(
        _outproj_body,
        grid=(T // tm,),
        in_specs=[
            pl.BlockSpec((tm, K1), lambda i: (i, 0)),
            pl.BlockSpec((tm, K2), lambda i: (i, 0)),
            pl.BlockSpec((K1, D), lambda i: (0, 0)),
            pl.BlockSpec((K2, D), lambda i: (1, 0)),
            pl.BlockSpec((tm, D), lambda i: (i, 0)),
            pl.BlockSpec((1, D), lambda i: (0, 0)),
        ],
        out_specs=[pl.BlockSpec((tm, D), lambda i: (i, 0)), pl.BlockSpec((tm, D), lambda i: (i, 0))],
        out_shape=[jax.ShapeDtypeStruct((T, D), F32), jax.ShapeDtypeStruct((T, D), BF16)],
        compiler_params=_params(("parallel",), 56),
        name="out_projection",
    )(a1, a2, w, w, res, g.reshape(1, D).astype(F32))


def _ffn_up_body(h_ref, wg_ref, wu_ref, cg_ref, cu_ref, o_ref):
    h = h_ref[...]
    gate = _causal_conv3(_dot(h, wg_ref[...]), cg_ref[...])
    up = _causal_conv3(_dot(h, wu_ref[...]), cu_ref[...])
    o_ref[...] = (gate * (1.0 / (1.0 + jnp.exp(-gate))) * up).astype(o_ref.dtype)


def ffn_up(h, w_up, conv_w, *, seq, tn):
    T, D = h.shape
    F = w_up.shape[1] // 2
    tn = _tile(F, tn)
    assert T % seq == 0
    nf = F // tn
    return pl.pallas_call(
        _ffn_up_body,
        grid=(T // seq, nf),
        in_specs=[
            pl.BlockSpec((seq, D), lambda b, j: (b, 0)),
            pl.BlockSpec((D, tn), lambda b, j: (0, j)),
            pl.BlockSpec((D, tn), lambda b, j: (0, nf + j)),
            pl.BlockSpec((3, tn), lambda b, j: (0, j)),
            pl.BlockSpec((3, tn), lambda b, j: (0, nf + j)),
        ],
        out_specs=pl.BlockSpec((seq, tn), lambda b, j: (b, j)),
        out_shape=jax.ShapeDtypeStruct((T, F), BF16),
        compiler_params=_params(("parallel", "arbitrary"), 56),
        name="ffn_up",
    )(h, w_up, w_up, conv_w, conv_w)


def _short_conv_body(gb_ref, gc_ref, hin_ref, cw_ref, o_ref):
    x = gc_ref[...].astype(F32) * hin_ref[...].astype(F32)
    o_ref[...] = (gb_ref[...].astype(F32) * _causal_conv3(x, cw_ref[...])).astype(o_ref.dtype)


def gated_short_conv(p, conv_w, *, seq, col0, width):
    T = p.shape[0]
    tc = min(width, 256)
    assert col0 % tc == 0 and width % tc == 0
    c0, nw = col0 // tc, width // tc
    return pl.pallas_call(
        _short_conv_body,
        grid=(T // seq, nw),
        in_specs=[
            pl.BlockSpec((seq, tc), lambda b, j: (b, c0 + j)),
            pl.BlockSpec((seq, tc), lambda b, j: (b, c0 + nw + j)),
            pl.BlockSpec((seq, tc), lambda b, j: (b, c0 + 2 * nw + j)),
            pl.BlockSpec((3, tc), lambda b, j: (0, j)),
        ],
        out_specs=pl.BlockSpec((seq, tc), lambda b, j: (b, j)),
        out_shape=jax.ShapeDtypeStruct((T, width), BF16),
        compiler_params=_params(("parallel", "parallel"), 32),
        name="gated_short_conv",
    )(p, p, p, conv_w)


def _sb_attention_body(q_ref, k_ref, v_ref, o_ref, acc_ref, r_ref, *, n_heads, scale):
    blk = ATT_BLOCK
    i = pl.program_id(1)
    row = lax.broadcasted_iota(jnp.int32, (blk, blk), 0)
    col = lax.broadcasted_iota(jnp.int32, (blk, blk), 1)
    strict = col < row
    suffix = jnp.where(row >= col, 1.0, 0.0).astype(BF16)

    def block(j, diagonal):
        start = pl.multiple_of(j * blk, blk)
        heads = [slice(h * HEAD_DIM, (h + 1) * HEAD_DIM) for h in range(n_heads)]
        zs = [_dot_nt(q_ref[:, hs], k_ref[pl.ds(start, blk), hs]) * scale for hs in heads]
        css = []
        for z in zs:
            sp = _softplus(z)
            if diagonal:
                sp = jnp.where(strict, sp, 0.0)
            sp_hi = sp.astype(BF16)
            sp_lo = (sp - sp_hi.astype(F32)).astype(BF16)
            css.append(_dot(sp_hi, suffix) + _dot(sp_lo, suffix))
        r_min = None
        for h, hs in enumerate(heads):
            z, cs = zs[h], css[h]
            vb = v_ref[pl.ds(start, blk), hs]
            if diagonal:
                a = jnp.where(strict, jnp.exp(z - cs), 0.0)
                r = cs[:, 0:1]
                acc_ref[h] = _dot(a.astype(BF16), vb)
            else:
                r = r_ref[h]
                a = jnp.exp(z - (cs + r))
                r = r + cs[:, 0:1]
                acc_ref[h] += _dot(a.astype(BF16), vb)
            r_ref[h] = r
            r_min = r if r_min is None else jnp.minimum(r_min, r)
        return jnp.min(r_min)

    def cond(carry):
        return jnp.logical_and(carry[0] >= 0, carry[1] < DEAD_EXPONENT)

    def body(carry):
        return carry[0] - 1, block(carry[0], False)

    lax.while_loop(cond, body, (i - 1, block(i, True)))
    for h in range(n_heads):
        o_ref[:, h * HEAD_DIM:(h + 1) * HEAD_DIM] = acc_ref[h].astype(o_ref.dtype)


def sb_attention(p, *, seq, n_heads):
    T = p.shape[0]
    W = n_heads * HEAD_DIM
    blk = ATT_BLOCK
    assert seq % blk == 0
    nq = seq // blk
    body = functools.partial(_sb_attention_body, n_heads=n_heads, scale=HEAD_DIM ** -0.5)
    return pl.pallas_call(
        body,
        grid=(T // seq, nq),
        in_specs=[
            pl.BlockSpec((blk, W), lambda b, i: (b * nq + i, 0)),
            pl.BlockSpec((seq, W), lambda b, i: (b, 1)),
            pl.BlockSpec((seq, W), lambda b, i: (b, 2)),
        ],
        out_specs=pl.BlockSpec((blk, W), lambda b, i: (b * nq + i, 0)),
        out_shape=jax.ShapeDtypeStruct((T, W), BF16),
        scratch_shapes=[pltpu.VMEM((n_heads, blk, HEAD_DIM), F32), pltpu.VMEM((n_heads, blk, 1), F32)],
        compiler_params=_params(("parallel", "arbitrary"), 48),
        name="sb_attention",
    )(p, p, p)


def _forget_cumsum_body(f_ref, b_ref, c_ref):
    x = f_ref[0] + b_ref[...]
    lf = -_softplus(-x)
    S = lf.shape[1]
    lane = lax.broadcasted_iota(jnp.int32, lf.shape, 1)
    k = 1
    while k < S:
        lf = lf + jnp.where(lane >= k, pltpu.roll(lf, k, 1), 0.0)
        k *= 2
    c_ref[0] = lf


def forget_cumsum(f, bias):
    B, H, S = f.shape
    return pl.pallas_call(
        _forget_cumsum_body,
        grid=(B,),
        in_specs=[pl.BlockSpec((1, H, S), lambda b: (b, 0, 0)), pl.BlockSpec((H, 1), lambda b: (0, 0))],
        out_specs=pl.BlockSpec((1, H, S), lambda b: (b, 0, 0)),
        out_shape=jax.ShapeDtypeStruct((B, H, S), F32),
        compiler_params=_params(("parallel",), 16),
        name="forget_cumsum",
    )(f, bias.reshape(H, 1).astype(F32))


def _fox_attention_body(q_ref, k_ref, v_ref, c_ref, o_ref, acc_ref, m_ref, l_ref, kmax_ref, *, n_heads, scale):
    blk = ATT_BLOCK
    i = pl.program_id(1)
    row = lax.broadcasted_iota(jnp.int32, (blk, blk), 0)
    col = lax.broadcasted_iota(jnp.int32, (blk, blk), 1)
    causal = col <= row
    heads = [slice(h * HEAD_DIM, (h + 1) * HEAD_DIM) for h in range(n_heads)]

    @pl.when(i == 0)
    def _():
        ones = jnp.ones((8, HEAD_DIM), BF16)
        for h, hs in enumerate(heads):
            k = k_ref[:, hs].astype(F32)
            n2 = _dot_nt(ones, (k * k).astype(BF16))[0:1, :]
            kmax_ref[h:h + 1, :] = jnp.broadcast_to(
                jnp.sqrt(jnp.max(n2, axis=-1, keepdims=True)) * NORM_SLACK, (1, HEAD_DIM))

    qk_bound = []
    for h, hs in enumerate(heads):
        q = q_ref[:, hs].astype(F32)
        qn2 = jnp.max(jnp.sum(q * q, axis=-1, keepdims=True), axis=0, keepdims=True)
        qk_bound.append(jnp.sqrt(qn2) * kmax_ref[h:h + 1, 0:1] * scale)

    def block(j, diagonal):
        start = pl.multiple_of(j * blk, blk)
        c_blk = c_ref[0, pl.ds(j, 1)][0]
        reach = None
        scores = [_dot_nt(q_ref[:, hs], k_ref[pl.ds(start, blk), hs]) for hs in heads]
        for h, hs in enumerate(heads):
            vb = v_ref[pl.ds(start, blk), hs]
            s = scores[h] * scale - c_blk[h:h + 1, :]
            if diagonal:
                s = jnp.where(causal, s, -jnp.inf)
                m = jnp.max(s, axis=-1, keepdims=True)
                p = jnp.exp(s - m)
                l_ref[h] = jnp.sum(p, axis=-1, keepdims=True)
                acc_ref[h] = _dot(p.astype(BF16), vb)
            else:
                m_old = m_ref[h]
                m = jnp.maximum(m_old, jnp.max(s, axis=-1, keepdims=True))
                alpha = jnp.exp(m_old - m)
                p = jnp.exp(s - m)
                l_ref[h] = alpha * l_ref[h] + jnp.sum(p, axis=-1, keepdims=True)
                acc_ref[h] = alpha * acc_ref[h] + _dot(p.astype(BF16), vb)
            m_ref[h] = m
            r = (qk_bound[h] - c_blk[h:h + 1, 0:1]) - m
            reach = r if reach is None else jnp.maximum(reach, r)
        return jnp.max(reach)

    def cond(carry):
        return jnp.logical_and(carry[0] >= 0, carry[1] > -DEAD_EXPONENT)

    def body(carry):
        return carry[0] - 1, block(carry[0], False)

    lax.while_loop(cond, body, (i - 1, block(i, True)))
    for h, hs in enumerate(heads):
        o_ref[:, hs] = (acc_ref[h] / l_ref[h]).astype(o_ref.dtype)


def fox_attention(p, c, *, seq, n_heads, col_block):
    T = p.shape[0]
    B = T // seq
    W = n_heads * HEAD_DIM
    blk = ATT_BLOCK
    nq = seq // blk
    c_blocks = c.reshape(B, n_heads, nq, blk).transpose(0, 2, 1, 3)
    body = functools.partial(_fox_attention_body, n_heads=n_heads, scale=HEAD_DIM ** -0.5)
    return pl.pallas_call(
        body,
        grid=(B, nq),
        in_specs=[
            pl.BlockSpec((blk, W), lambda b, i: (b * nq + i, col_block)),
            pl.BlockSpec((seq, W), lambda b, i: (b, col_block + 1)),
            pl.BlockSpec((seq, W), lambda b, i: (b, col_block + 2)),
            pl.BlockSpec((1, nq, n_heads, blk), lambda b, i: (b, 0, 0, 0)),
        ],
        out_specs=pl.BlockSpec((blk, W), lambda b, i: (b * nq + i, 0)),
        out_shape=jax.ShapeDtypeStruct((T, W), BF16),
        scratch_shapes=[
            pltpu.VMEM((n_heads, blk, HEAD_DIM), F32),
            pltpu.VMEM((n_heads, blk, 1), F32),
            pltpu.VMEM((n_heads, blk, 1), F32),
            pltpu.VMEM((n_heads, HEAD_DIM), F32),
        ],
        compiler_params=_params(("arbitrary", "arbitrary"), 48),
        name="fox_attention",
    )(p, p, p, c_blocks)


def _spatial_gate_body(u_ref, v_ref, w_ref, bt_ref, g_ref, o_ref, *, n_groups):
    tm = u_ref.shape[0]
    v = _gelu_tanh(v_ref[...].astype(F32))
    mu = jnp.mean(v, axis=-1, keepdims=True)
    vc = v - mu
    var = jnp.mean(vc * vc, axis=-1, keepdims=True)
    vn = (vc * lax.rsqrt(var + EPS) * g_ref[...]).astype(BF16)
    row = lax.broadcasted_iota(jnp.int32, (CHUNK, CHUNK), 0)
    col = lax.broadcasted_iota(jnp.int32, (CHUNK, CHUNK), 1)
    for g in range(n_groups):
        gs = slice(g * SG_GROUP_DIM, (g + 1) * SG_GROUP_DIM)
        w = jnp.where(col <= row, w_ref[g], 0.0).astype(BF16)
        b = bt_ref[:, g:g + 1]
        for c in range(tm // CHUNK):
            cs = slice(c * CHUNK, (c + 1) * CHUNK)
            mixed = _dot(w, vn[cs, gs]) + b
            o_ref[cs, gs] = (_gelu_tanh(u_ref[cs, gs].astype(F32)) * mixed).astype(o_ref.dtype)


def spatial_gate(p, sg_w, sg_b, sg_g, *, width):
    T = p.shape[0]
    G = width // SG_GROUP_DIM
    tm = min(T, 512)
    assert tm % CHUNK == 0
    body = functools.partial(_spatial_gate_body, n_groups=G)
    return pl.pallas_call(
        body,
        grid=(T // tm,),
        in_specs=[
            pl.BlockSpec((tm, width), lambda i: (i, 0)),
            pl.BlockSpec((tm, width), lambda i: (i, 1)),
            pl.BlockSpec((G, CHUNK, CHUNK), lambda i: (0, 0, 0)),
            pl.BlockSpec((CHUNK, G), lambda i: (0, 0)),
            pl.BlockSpec((1, width), lambda i: (0, 0)),
        ],
        out_specs=pl.BlockSpec((tm, width), lambda i: (i, 0)),
        out_shape=jax.ShapeDtypeStruct((T, width), BF16),
        compiler_params=_params(("parallel",), 32),
        name="spatial_gate",
    )(p, p, sg_w, sg_b.T, sg_g.reshape(1, width).astype(F32))


def _conv_ffn(x, h, w_up, conv_w, w_down, seq):
    g = ffn_up(h, w_up.astype(BF16), conv_w, seq=seq, tn=256)
    return matmul(g, w_down.astype(BF16), x, tm=1024, tn=512, out_dtype=F32, name="ffn_down")


def kernel(x, l0_mix_norm_g, l0_w_in, l0_sc_conv_w, l0_w_out, l0_ffn_norm_g, l0_ffn_up, l0_ffn_conv_w, l0_ffn_down, l1_mix_norm_g, l1_w_in, l1_fox_b_f, l1_sg_w, l1_sg_b, l1_sg_norm_g, l1_w_out, l1_ffn_norm_g, l1_ffn_up, l1_ffn_conv_w, l1_ffn_down, final_norm_g):
    B, S, D = x.shape
    T = B * S
    half = D // 2
    n_heads = half // HEAD_DIM
    x = x.reshape(T, D)

    h = rmsnorm(x, l0_mix_norm_g, BF16)
    p = matmul(h, l0_w_in.astype(BF16), tm=1024, tn=1024, out_dtype=BF16, name="in_projection0")
    a_out = sb_attention(p, seq=S, n_heads=n_heads)
    b_out = gated_short_conv(p, l0_sc_conv_w, seq=S, col0=3 * half, width=half)
    x, h = out_projection(a_out, b_out, l0_w_out.astype(BF16), x, l0_ffn_norm_g)
    x = _conv_ffn(x, h, l0_ffn_up, l0_ffn_conv_w, l0_ffn_down, S)

    h = rmsnorm(x, l1_mix_norm_g, BF16)
    n_main = 5 * half
    p = matmul(h, l1_w_in[:, :n_main].astype(BF16), tm=1024, tn=1024, out_dtype=BF16, name="in_projection1")
    w_f = jnp.pad(l1_w_in[:, n_main:], ((0, 0), (0, HEAD_DIM - n_heads))).astype(BF16)
    f = matmul(h, w_f, tm=1024, tn=HEAD_DIM, out_dtype=F32, name="forget_projection")
    f = f[:, :n_heads].reshape(B, S, n_heads).transpose(0, 2, 1)
    c = forget_cumsum(f, l1_fox_b_f)
    c_out = spatial_gate(p, l1_sg_w, l1_sg_b, l1_sg_norm_g, width=half)
    d_out = fox_attention(p, c, seq=S, n_heads=n_heads, col_block=2)
    x, h = out_projection(c_out, d_out, l1_w_out.astype(BF16), x, l1_ffn_norm_g)
    x = _conv_ffn(x, h, l1_ffn_up, l1_ffn_conv_w, l1_ffn_down, S)

    return rmsnorm(x, final_norm_g, F32).reshape(B, S, D)
```

```python
import functools

import jax
import jax.numpy as jnp
from jax import lax
from jax.experimental import pallas as pl
from jax.experimental.pallas import tpu as pltpu

EPS = 1e-6
HEAD_DIM = 128
CHUNK = 128
SG_GROUP_DIM = 128
ATT_BLOCK = 256
DEAD_EXPONENT = 105.0
NORM_SLACK = 1.01

V7X_VMEM_BYTES = 64 * 1024 * 1024
MIB = 1024 * 1024

F32 = jnp.float32
BF16 = jnp.bfloat16


def _params(semantics, vmem_mib, flags=None):
    assert vmem_mib * MIB < V7X_VMEM_BYTES
    return pltpu.CompilerParams(dimension_semantics=semantics, vmem_limit_bytes=vmem_mib * MIB, flags=flags)


def _tile(n, target, quantum=128):
    if n <= target:
        return n
    t = (target // quantum) * quantum
    while n % t:
        t -= quantum
    return t


def _dot(a, b):
    return jnp.dot(a, b, preferred_element_type=F32)


def _dot_nt(a, b):
    return lax.dot_general(a, b, (((1,), (1,)), ((), ())), preferred_element_type=F32)


def _softplus(z):
    return jnp.maximum(z, 0.0) + jnp.log(1.0 + jnp.exp(-jnp.abs(z)))


def _gelu_tanh(x):
    return 0.5 * x * (1.0 + jnp.tanh(0.7978845608028654 * (x + 0.044715 * (x * x * x))))


def _shift_rows(a, k):
    rolled = pltpu.roll(a, k, 0)
    row = lax.broadcasted_iota(jnp.int32, a.shape, 0)
    return jnp.where(row >= k, rolled, 0.0)


def _causal_conv3(a, cw):
    return _shift_rows(a, 2) * cw[0:1, :] + _shift_rows(a, 1) * cw[1:2, :] + a * cw[2:3, :]


def _rmsnorm_body(x_ref, g_ref, o_ref):
    x = x_ref[...]
    ms = jnp.mean(x * x, axis=-1, keepdims=True)
    o_ref[...] = (x * lax.rsqrt(ms + EPS) * g_ref[...]).astype(o_ref.dtype)


def rmsnorm(x, g, out_dtype):
    T, D = x.shape
    tm = min(T, 512)
    return pl.pallas_call(
        _rmsnorm_body,
        grid=(T // tm,),
        in_specs=[pl.BlockSpec((tm, D), lambda i: (i, 0)), pl.BlockSpec((1, D), lambda i: (0, 0))],
        out_specs=pl.BlockSpec((tm, D), lambda i: (i, 0)),
        out_shape=jax.ShapeDtypeStruct((T, D), out_dtype),
        compiler_params=_params(("parallel",), 40),
        name="rmsnorm",
    )(x, g.reshape(1, D).astype(F32))


def _mm_body(a_ref, w_ref, o_ref):
    o_ref[...] = _dot(a_ref[...], w_ref[...]).astype(o_ref.dtype)


def _mm_res_body(a_ref, w_ref, r_ref, o_ref):
    o_ref[...] = r_ref[...] + _dot(a_ref[...], w_ref[...])


def matmul(a, w, res=None, *, tm, tn, out_dtype, name):
    M, K = a.shape
    N = w.shape[1]
    tm, tn = _tile(M, tm), _tile(N, tn)
    in_specs = [pl.BlockSpec((tm, K), lambda i, j: (i, 0)), pl.BlockSpec((K, tn), lambda i, j: (0, j))]
    args = [a, w]
    body = _mm_body
    if res is not None:
        in_specs.append(pl.BlockSpec((tm, tn), lambda i, j: (i, j)))
        args.append(res)
        body = _mm_res_body
    return pl.pallas_call(
        body,
        grid=(M // tm, N // tn),
        in_specs=in_specs,
        out_specs=pl.BlockSpec((tm, tn), lambda i, j: (i, j)),
        out_shape=jax.ShapeDtypeStruct((M, N), out_dtype),
        compiler_params=_params(("parallel", "arbitrary"), 56),
        name=name,
    )(*args)


def _in_proj_body(x_ref, g_ref, w_ref, *rest, with_extra):
    if with_extra:
        we_ref, o_ref, e_ref, h_s = rest
    else:
        o_ref, h_s = rest

    @pl.when(pl.program_id(1) == 0)
    def _():
        x = x_ref[...]
        ms = jnp.mean(x * x, axis=-1, keepdims=True)
        h_s[...] = (x * lax.rsqrt(ms + EPS) * g_ref[...]).astype(BF16)
        if with_extra:
            e_ref[...] = _dot(h_s[...], we_ref[...])

    o_ref[...] = _dot(h_s[...], w_ref[...]).astype(o_ref.dtype)


def in_projection(x, g, w, w_extra=None, *, tm, tn, name):
    T, D = x.shape
    N = w.shape[1]
    tm, tn = _tile(T, tm), _tile(N, tn)
    in_specs = [
        pl.BlockSpec((tm, D), lambda i, j: (i, 0)),
        pl.BlockSpec((1, D), lambda i, j: (0, 0)),
        pl.BlockSpec((D, tn), lambda i, j: (0, j)),
    ]
    args = [x, g.reshape(1, D).astype(F32), w]
    out_specs = [pl.BlockSpec((tm, tn), lambda i, j: (i, j))]
    out_shape = [jax.ShapeDtypeStruct((T, N), BF16)]
    if w_extra is not None:
        ne = w_extra.shape[1]
        in_specs.append(pl.BlockSpec((D, ne), lambda i, j: (0, 0)))
        args.append(w_extra)
        out_specs.append(pl.BlockSpec((tm, ne), lambda i, j: (i, 0)))
        out_shape.append(jax.ShapeDtypeStruct((T, ne), F32))
    outs = pl.pallas_call(
        functools.partial(_in_proj_body, with_extra=w_extra is not None),
        grid=(T // tm, N // tn),
        in_specs=in_specs,
        out_specs=out_specs,
        out_shape=out_shape,
        scratch_shapes=[pltpu.VMEM((tm, D), BF16)],
        compiler_params=_params(("parallel", "arbitrary"), 56),
        name=name,
    )(*args)
    return outs if w_extra is not None else outs[0]


def _outproj_body(a1_ref, a2_ref, w1_ref, w2_ref, r_ref, g_ref, x_ref, h_ref):
    x = r_ref[...] + _dot(a1_ref[...], w1_ref[...]) + _dot(a2_ref[...], w2_ref[...])
    x_ref[...] = x
    ms = jnp.mean(x * x, axis=-1, keepdims=True)
    h_ref[...] = (x * lax.rsqrt(ms + EPS) * g_ref[...]).astype(h_ref.dtype)


def out_projection(a1, a2, w, res, g):
    T, K1 = a1.shape
    K2 = a2.shape[1]
    D = w.shape[1]
    assert K1 == K2 and w.shape[0] == K1 + K2
    tm = min(T, 512)
    return pl.pallas_call(
        _outproj_body,
        grid=(T // tm,),
        in_specs=[
            pl.BlockSpec((tm, K1), lambda i: (i, 0)),
            pl.BlockSpec((tm, K2), lambda i: (i, 0)),
            pl.BlockSpec((K1, D), lambda i: (0, 0)),
            pl.BlockSpec((K2, D), lambda i: (1, 0)),
            pl.BlockSpec((tm, D), lambda i: (i, 0)),
            pl.BlockSpec((1, D), lambda i: (0, 0)),
        ],
        out_specs=[pl.BlockSpec((tm, D), lambda i: (i, 0)), pl.BlockSpec((tm, D), lambda i: (i, 0))],
        out_shape=[jax.ShapeDtypeStruct((T, D), F32), jax.ShapeDtypeStruct((T, D), BF16)],
        compiler_params=_params(("parallel",), 56),
        name="out_projection",
    )(a1, a2, w, w, res, g.reshape(1, D).astype(F32))


FFN_PIECE = 128


def _ffn_up_body(h_ref, wg_ref, wu_ref, cg_ref, cu_ref, o_ref, *, row_chunk):
    rows = h_ref.shape[0]
    pieces = [slice(c0, c0 + FFN_PIECE) for c0 in range(0, o_ref.shape[1], FFN_PIECE)]
    w = jnp.concatenate([ref[:, cs] for cs in pieces for ref in (wg_ref, wu_ref)], axis=1)
    cw = jnp.concatenate([ref[:, cs] for cs in pieces for ref in (cg_ref, cu_ref)], axis=1)
    a = jnp.concatenate([_dot(h_ref[r:r + row_chunk, :], w) for r in range(0, rows, row_chunk)], axis=0)
    y = _causal_conv3(a, cw)
    for k, cs in enumerate(pieces):
        gate = y[:, 2 * k * FFN_PIECE:(2 * k + 1) * FFN_PIECE]
        up = y[:, (2 * k + 1) * FFN_PIECE:(2 * k + 2) * FFN_PIECE]
        o_ref[:, cs] = (gate * (1.0 / (1.0 + jnp.exp(-gate))) * up).astype(o_ref.dtype)


def ffn_up(h, w_up, conv_w, *, seq, tn):
    T, D = h.shape
    F = w_up.shape[1] // 2
    tn = _tile(F, tn)
    assert T % seq == 0 and tn % FFN_PIECE == 0
    nf = F // tn
    body = functools.partial(_ffn_up_body, row_chunk=_tile(seq, 1024))
    return pl.pallas_call(
        body,
        grid=(T // seq, nf),
        in_specs=[
            pl.BlockSpec((seq, D), lambda b, j: (b, 0)),
            pl.BlockSpec((D, tn), lambda b, j: (0, j)),
            pl.BlockSpec((D, tn), lambda b, j: (0, nf + j)),
            pl.BlockSpec((3, tn), lambda b, j: (0, j)),
            pl.BlockSpec((3, tn), lambda b, j: (0, nf + j)),
        ],
        out_specs=pl.BlockSpec((seq, tn), lambda b, j: (b, j)),
        out_shape=jax.ShapeDtypeStruct((T, F), BF16),
        compiler_params=_params(("parallel", "arbitrary"), 56),
        name="ffn_up",
    )(h, w_up, w_up, conv_w, conv_w)


def _short_conv_body(gb_ref, gc_ref, hin_ref, cw_ref, o_ref):
    x = gc_ref[...].astype(F32) * hin_ref[...].astype(F32)
    o_ref[...] = (gb_ref[...].astype(F32) * _causal_conv3(x, cw_ref[...])).astype(o_ref.dtype)


def gated_short_conv(p, conv_w, *, seq, col0, width):
    T = p.shape[0]
    tc = min(width, 256)
    assert col0 % tc == 0 and width % tc == 0
    c0, nw = col0 // tc, width // tc
    return pl.pallas_call(
        _short_conv_body,
        grid=(T // seq, nw),
        in_specs=[
            pl.BlockSpec((seq, tc), lambda b, j: (b, c0 + j)),
            pl.BlockSpec((seq, tc), lambda b, j: (b, c0 + nw + j)),
            pl.BlockSpec((seq, tc), lambda b, j: (b, c0 + 2 * nw + j)),
            pl.BlockSpec((3, tc), lambda b, j: (0, j)),
        ],
        out_specs=pl.BlockSpec((seq, tc), lambda b, j: (b, j)),
        out_shape=jax.ShapeDtypeStruct((T, width), BF16),
        compiler_params=_params(("parallel", "parallel"), 32),
        name="gated_short_conv",
    )(p, p, p, conv_w)


def _sb_attention_body(q_ref, k_ref, v_ref, o_ref, acc_ref, r_ref, *, n_heads, scale):
    blk = ATT_BLOCK
    i = pl.program_id(1)
    row = lax.broadcasted_iota(jnp.int32, (blk, blk), 0)
    col = lax.broadcasted_iota(jnp.int32, (blk, blk), 1)
    strict = col < row
    suffix = jnp.where(row >= col, 1.0, 0.0).astype(BF16)

    def block(j, diagonal):
        start = pl.multiple_of(j * blk, blk)
        heads = [slice(h * HEAD_DIM, (h + 1) * HEAD_DIM) for h in range(n_heads)]
        zs = [_dot_nt(q_ref[:, hs], k_ref[pl.ds(start, blk), hs]) * scale for hs in heads]
        css = []
        for z in zs:
            sp = _softplus(z)
            if diagonal:
                sp = jnp.where(strict, sp, 0.0)
            sp_hi = sp.astype(BF16)
            sp_lo = (sp - sp_hi.astype(F32)).astype(BF16)
            css.append(_dot(sp_hi, suffix) + _dot(sp_lo, suffix))
        r_min = None
        for h, hs in enumerate(heads):
            z, cs = zs[h], css[h]
            vb = v_ref[pl.ds(start, blk), hs]
            if diagonal:
                a = jnp.where(strict, jnp.exp(z - cs), 0.0)
                r = cs[:, 0:1]
                acc_ref[h] = _dot(a.astype(BF16), vb)
            else:
                r = r_ref[h]
                a = jnp.exp(z - (cs + r))
                r = r + cs[:, 0:1]
                acc_ref[h] += _dot(a.astype(BF16), vb)
            r_ref[h] = r
            r_min = r if r_min is None else jnp.minimum(r_min, r)
        return jnp.min(r_min)

    def cond(carry):
        return jnp.logical_and(carry[0] >= 0, carry[1] < DEAD_EXPONENT)

    def body(carry):
        return carry[0] - 1, block(carry[0], False)

    lax.while_loop(cond, body, (i - 1, block(i, True)))
    for h in range(n_heads):
        o_ref[:, h * HEAD_DIM:(h + 1) * HEAD_DIM] = acc_ref[h].astype(o_ref.dtype)


def sb_attention(p, *, seq, n_heads):
    T = p.shape[0]
    W = n_heads * HEAD_DIM
    blk = ATT_BLOCK
    assert seq % blk == 0
    nq = seq // blk
    body = functools.partial(_sb_attention_body, n_heads=n_heads, scale=HEAD_DIM ** -0.5)
    return pl.pallas_call(
        body,
        grid=(T // seq, nq),
        in_specs=[
            pl.BlockSpec((blk, W), lambda b, i: (b * nq + i, 0)),
            pl.BlockSpec((seq, W), lambda b, i: (b, 1)),
            pl.BlockSpec((seq, W), lambda b, i: (b, 2)),
        ],
        out_specs=pl.BlockSpec((blk, W), lambda b, i: (b * nq + i, 0)),
        out_shape=jax.ShapeDtypeStruct((T, W), BF16),
        scratch_shapes=[pltpu.VMEM((n_heads, blk, HEAD_DIM), F32), pltpu.VMEM((n_heads, blk, 1), F32)],
        compiler_params=_params(("parallel", "arbitrary"), 48),
        name="sb_attention",
    )(p, p, p)


def _forget_cumsum_body(f_ref, b_ref, c_ref):
    x = f_ref[0] + b_ref[...]
    lf = -_softplus(-x)
    S = lf.shape[1]
    lane = lax.broadcasted_iota(jnp.int32, lf.shape, 1)
    k = 1
    while k < S:
        lf = lf + jnp.where(lane >= k, pltpu.roll(lf, k, 1), 0.0)
        k *= 2
    c_ref[0] = lf


def forget_cumsum(f, bias):
    B, H, S = f.shape
    return pl.pallas_call(
        _forget_cumsum_body,
        grid=(B,),
        in_specs=[pl.BlockSpec((1, H, S), lambda b: (b, 0, 0)), pl.BlockSpec((H, 1), lambda b: (0, 0))],
        out_specs=pl.BlockSpec((1, H, S), lambda b: (b, 0, 0)),
        out_shape=jax.ShapeDtypeStruct((B, H, S), F32),
        compiler_params=_params(("parallel",), 16),
        name="forget_cumsum",
    )(f, bias.reshape(H, 1).astype(F32))


def _fox_attention_body(q_ref, k_ref, v_ref, c_ref, o_ref, acc_ref, m_ref, kmax_ref, *, n_heads, scale):
    blk = ATT_BLOCK
    i = pl.program_id(1)
    row = lax.broadcasted_iota(jnp.int32, (blk, blk), 0)
    col = lax.broadcasted_iota(jnp.int32, (blk, blk), 1)
    causal = col <= row
    heads = [slice(h * HEAD_DIM, (h + 1) * HEAD_DIM) for h in range(n_heads)]
    ones_blk = jnp.ones((blk, HEAD_DIM), BF16)

    @pl.when(i == 0)
    def _():
        ones = jnp.ones((8, HEAD_DIM), BF16)
        for h, hs in enumerate(heads):
            k = k_ref[:, hs].astype(F32)
            n2 = _dot_nt(ones, (k * k).astype(BF16))[0:1, :]
            kmax_ref[h:h + 1, :] = jnp.broadcast_to(
                jnp.sqrt(jnp.max(n2, axis=-1, keepdims=True)) * NORM_SLACK, (1, HEAD_DIM))

    qk_bound = []
    for h, hs in enumerate(heads):
        q = q_ref[:, hs].astype(F32)
        qn2 = jnp.max(jnp.sum(q * q, axis=-1, keepdims=True), axis=0, keepdims=True)
        qk_bound.append(jnp.sqrt(qn2) * kmax_ref[h:h + 1, 0:1] * scale)

    def block(j, diagonal):
        start = pl.multiple_of(j * blk, blk)
        c_blk = c_ref[0, pl.ds(j, 1)][0]
        reach = None
        scores = [_dot_nt(q_ref[:, hs], k_ref[pl.ds(start, blk), hs]) for hs in heads]
        for h, hs in enumerate(heads):
            vb = jnp.concatenate([v_ref[pl.ds(start, blk), hs], ones_blk], axis=1)
            s = scores[h] * scale - c_blk[h:h + 1, :]
            if diagonal:
                s = jnp.where(causal, s, -jnp.inf)
                m = jnp.max(s, axis=-1, keepdims=True)
                acc_ref[h] = _dot(jnp.exp(s - m).astype(BF16), vb)
            else:
                m_old = m_ref[h]
                m = jnp.maximum(m_old, jnp.max(s, axis=-1, keepdims=True))
                acc_ref[h] = jnp.exp(m_old - m) * acc_ref[h] + _dot(jnp.exp(s - m).astype(BF16), vb)
            m_ref[h] = m
            r = (qk_bound[h] - c_blk[h:h + 1, 0:1]) - m
            reach = r if reach is None else jnp.maximum(reach, r)
        return jnp.max(reach)

    def cond(carry):
        return jnp.logical_and(carry[0] >= 0, carry[1] > -DEAD_EXPONENT)

    def body(carry):
        return carry[0] - 1, block(carry[0], False)

    lax.while_loop(cond, body, (i - 1, block(i, True)))
    for h, hs in enumerate(heads):
        o_ref[:, hs] = (acc_ref[h, :, 0:HEAD_DIM] / acc_ref[h, :, HEAD_DIM:]).astype(o_ref.dtype)


def fox_attention(p, c, *, seq, n_heads, col_block):
    T = p.shape[0]
    B = T // seq
    W = n_heads * HEAD_DIM
    blk = ATT_BLOCK
    nq = seq // blk
    c_blocks = c.reshape(B, n_heads, nq, blk).transpose(0, 2, 1, 3)
    body = functools.partial(_fox_attention_body, n_heads=n_heads, scale=HEAD_DIM ** -0.5)
    return pl.pallas_call(
        body,
        grid=(B, nq),
        in_specs=[
            pl.BlockSpec((blk, W), lambda b, i: (b * nq + i, col_block)),
            pl.BlockSpec((seq, W), lambda b, i: (b, col_block + 1)),
            pl.BlockSpec((seq, W), lambda b, i: (b, col_block + 2)),
            pl.BlockSpec((1, nq, n_heads, blk), lambda b, i: (b, 0, 0, 0)),
        ],
        out_specs=pl.BlockSpec((blk, W), lambda b, i: (b * nq + i, 0)),
        out_shape=jax.ShapeDtypeStruct((T, W), BF16),
        scratch_shapes=[
            pltpu.VMEM((n_heads, blk, 2 * HEAD_DIM), F32),
            pltpu.VMEM((n_heads, blk, 1), F32),
            pltpu.VMEM((n_heads, HEAD_DIM), F32),
        ],
        compiler_params=_params(("arbitrary", "arbitrary"), 48),
        name="fox_attention",
    )(p, p, p, c_blocks)


def _spatial_gate_body(u_ref, v_ref, w_ref, bt_ref, g_ref, o_ref, *, n_groups):
    tm = u_ref.shape[0]
    v = _gelu_tanh(v_ref[...].astype(F32))
    mu = jnp.mean(v, axis=-1, keepdims=True)
    vc = v - mu
    var = jnp.mean(vc * vc, axis=-1, keepdims=True)
    vn = (vc * lax.rsqrt(var + EPS) * g_ref[...]).astype(BF16)
    row = lax.broadcasted_iota(jnp.int32, (CHUNK, CHUNK), 0)
    col = lax.broadcasted_iota(jnp.int32, (CHUNK, CHUNK), 1)
    for g in range(n_groups):
        gs = slice(g * SG_GROUP_DIM, (g + 1) * SG_GROUP_DIM)
        w = jnp.where(col <= row, w_ref[g], 0.0).astype(BF16)
        b = bt_ref[:, g:g + 1]
        for c in range(tm // CHUNK):
            cs = slice(c * CHUNK, (c + 1) * CHUNK)
            mixed = _dot(w, vn[cs, gs]) + b
            o_ref[cs, gs] = (_gelu_tanh(u_ref[cs, gs].astype(F32)) * mixed).astype(o_ref.dtype)


def spatial_gate(p, sg_w, sg_b, sg_g, *, width):
    T = p.shape[0]
    G = width // SG_GROUP_DIM
    tm = min(T, 512)
    assert tm % CHUNK == 0
    body = functools.partial(_spatial_gate_body, n_groups=G)
    return pl.pallas_call(
        body,
        grid=(T // tm,),
        in_specs=[
            pl.BlockSpec((tm, width), lambda i: (i, 0)),
            pl.BlockSpec((tm, width), lambda i: (i, 1)),
            pl.BlockSpec((G, CHUNK, CHUNK), lambda i: (0, 0, 0)),
            pl.BlockSpec((CHUNK, G), lambda i: (0, 0)),
            pl.BlockSpec((1, width), lambda i: (0, 0)),
        ],
        out_specs=pl.BlockSpec((tm, width), lambda i: (i, 0)),
        out_shape=jax.ShapeDtypeStruct((T, width), BF16),
        compiler_params=_params(("parallel",), 32),
        name="spatial_gate",
    )(p, p, sg_w, sg_b.T, sg_g.reshape(1, width).astype(F32))


def _conv_ffn(x, h, w_up, conv_w, w_down, seq):
    g = ffn_up(h, w_up.astype(BF16), conv_w, seq=seq, tn=512)
    return matmul(g, w_down.astype(BF16), x, tm=1024, tn=512, out_dtype=F32, name="ffn_down")


def kernel(x, l0_mix_norm_g, l0_w_in, l0_sc_conv_w, l0_w_out, l0_ffn_norm_g, l0_ffn_up, l0_ffn_conv_w, l0_ffn_down, l1_mix_norm_g, l1_w_in, l1_fox_b_f, l1_sg_w, l1_sg_b, l1_sg_norm_g, l1_w_out, l1_ffn_norm_g, l1_ffn_up, l1_ffn_conv_w, l1_ffn_down, final_norm_g):
    B, S, D = x.shape
    T = B * S
    half = D // 2
    n_heads = half // HEAD_DIM
    x = x.reshape(T, D)

    p = in_projection(x, l0_mix_norm_g, l0_w_in.astype(BF16), tm=1024, tn=1024, name="in_projection0")
    a_out = sb_attention(p, seq=S, n_heads=n_heads)
    b_out = gated_short_conv(p, l0_sc_conv_w, seq=S, col0=3 * half, width=half)
    x, h = out_projection(a_out, b_out, l0_w_out.astype(BF16), x, l0_ffn_norm_g)
    x = _conv_ffn(x, h, l0_ffn_up, l0_ffn_conv_w, l0_ffn_down, S)

    n_main = 5 * half
    w_f = jnp.pad(l1_w_in[:, n_main:], ((0, 0), (0, HEAD_DIM - n_heads))).astype(BF16)
    p, f = in_projection(x, l1_mix_norm_g, l1_w_in[:, :n_main].astype(BF16), w_f,
                         tm=1024, tn=1024, name="in_projection1")
    f = f[:, :n_heads].reshape(B, S, n_heads).transpose(0, 2, 1)
    c = forget_cumsum(f, l1_fox_b_f)
    c_out = spatial_gate(p, l1_sg_w, l1_sg_b, l1_sg_norm_g, width=half)
    d_out = fox_attention(p, c, seq=S, n_heads=n_heads, col_block=2)
    x, h = out_projection(c_out, d_out, l1_w_out.astype(BF16), x, l1_ffn_norm_g)
    x = _conv_ffn(x, h, l1_ffn_up, l1_ffn_conv_w, l1_ffn_down, S)

    return rmsnorm(x, final_norm_g, F32).reshape(B, S, D)
```

```python
import functools

import jax
import jax.numpy as jnp
from jax import lax
from jax.experimental import pallas as pl
from jax.experimental.pallas import tpu as pltpu

EPS = 1e-6
HEAD_DIM = 128
CHUNK = 128
SG_GROUP_DIM = 128
ATT_BLOCK = 256
DEAD_EXPONENT = 105.0
NORM_SLACK = 1.01

V7X_VMEM_BYTES = 64 * 1024 * 1024
MIB = 1024 * 1024

F32 = jnp.float32
BF16 = jnp.bfloat16


def _params(semantics, vmem_mib, flags=None):
    assert vmem_mib * MIB < V7X_VMEM_BYTES
    return pltpu.CompilerParams(dimension_semantics=semantics, vmem_limit_bytes=vmem_mib * MIB, flags=flags)


def _tile(n, target, quantum=128):
    if n <= target:
        return n
    t = (target // quantum) * quantum
    while n % t:
        t -= quantum
    return t


def _dot(a, b):
    return jnp.dot(a, b, preferred_element_type=F32)


def _dot_nt(a, b):
    return lax.dot_general(a, b, (((1,), (1,)), ((), ())), preferred_element_type=F32)


def _softplus(z):
    return jnp.maximum(z, 0.0) + jnp.log(1.0 + jnp.exp(-jnp.abs(z)))


def _gelu_tanh(x):
    return 0.5 * x * (1.0 + jnp.tanh(0.7978845608028654 * (x + 0.044715 * (x * x * x))))


def _shift_rows(a, k):
    rolled = pltpu.roll(a, k, 0)
    row = lax.broadcasted_iota(jnp.int32, a.shape, 0)
    return jnp.where(row >= k, rolled, 0.0)


def _causal_conv3(a, cw):
    return _shift_rows(a, 2) * cw[0:1, :] + _shift_rows(a, 1) * cw[1:2, :] + a * cw[2:3, :]


def _rmsnorm_body(x_ref, g_ref, o_ref):
    x = x_ref[...]
    ms = jnp.mean(x * x, axis=-1, keepdims=True)
    o_ref[...] = (x * lax.rsqrt(ms + EPS) * g_ref[...]).astype(o_ref.dtype)


def rmsnorm(x, g, out_dtype):
    T, D = x.shape
    tm = min(T, 512)
    return pl.pallas_call(
        _rmsnorm_body,
        grid=(T // tm,),
        in_specs=[pl.BlockSpec((tm, D), lambda i: (i, 0)), pl.BlockSpec((1, D), lambda i: (0, 0))],
        out_specs=pl.BlockSpec((tm, D), lambda i: (i, 0)),
        out_shape=jax.ShapeDtypeStruct((T, D), out_dtype),
        compiler_params=_params(("parallel",), 40),
        name="rmsnorm",
    )(x, g.reshape(1, D).astype(F32))


def _mm_body(a_ref, w_ref, o_ref):
    o_ref[...] = _dot(a_ref[...], w_ref[...]).astype(o_ref.dtype)


def _mm_res_body(a_ref, w_ref, r_ref, o_ref):
    o_ref[...] = r_ref[...] + _dot(a_ref[...], w_ref[...])


def matmul(a, w, res=None, *, tm, tn, out_dtype, name):
    M, K = a.shape
    N = w.shape[1]
    tm, tn = _tile(M, tm), _tile(N, tn)
    in_specs = [pl.BlockSpec((tm, K), lambda i, j: (i, 0)), pl.BlockSpec((K, tn), lambda i, j: (0, j))]
    args = [a, w]
    body = _mm_body
    if res is not None:
        in_specs.append(pl.BlockSpec((tm, tn), lambda i, j: (i, j)))
        args.append(res)
        body = _mm_res_body
    return pl.pallas_call(
        body,
        grid=(M // tm, N // tn),
        in_specs=in_specs,
        out_specs=pl.BlockSpec((tm, tn), lambda i, j: (i, j)),
        out_shape=jax.ShapeDtypeStruct((M, N), out_dtype),
        compiler_params=_params(("parallel", "arbitrary"), 56),
        name=name,
    )(*args)


def _in_proj_body(x_ref, g_ref, w_ref, *rest, with_extra):
    if with_extra:
        we_ref, o_ref, e_ref, h_s = rest
    else:
        o_ref, h_s = rest

    @pl.when(pl.program_id(1) == 0)
    def _():
        x = x_ref[...]
        ms = jnp.mean(x * x, axis=-1, keepdims=True)
        h_s[...] = (x * lax.rsqrt(ms + EPS) * g_ref[...]).astype(BF16)
        if with_extra:
            e_ref[...] = _dot(h_s[...], we_ref[...])

    o_ref[...] = _dot(h_s[...], w_ref[...].astype(BF16)).astype(o_ref.dtype)


def in_projection(x, g, w, w_extra=None, *, n_cols, tm, tn, name):
    T, D = x.shape
    N = n_cols
    tm, tn = _tile(T, tm), _tile(N, tn)
    in_specs = [
        pl.BlockSpec((tm, D), lambda i, j: (i, 0)),
        pl.BlockSpec((1, D), lambda i, j: (0, 0)),
        pl.BlockSpec((D, tn), lambda i, j: (0, j)),
    ]
    args = [x, g.reshape(1, D).astype(F32), w]
    out_specs = [pl.BlockSpec((tm, tn), lambda i, j: (i, j))]
    out_shape = [jax.ShapeDtypeStruct((T, N), BF16)]
    if w_extra is not None:
        ne = w_extra.shape[1]
        in_specs.append(pl.BlockSpec((D, ne), lambda i, j: (0, 0)))
        args.append(w_extra)
        out_specs.append(pl.BlockSpec((tm, ne), lambda i, j: (i, 0)))
        out_shape.append(jax.ShapeDtypeStruct((T, ne), F32))
    outs = pl.pallas_call(
        functools.partial(_in_proj_body, with_extra=w_extra is not None),
        grid=(T // tm, N // tn),
        in_specs=in_specs,
        out_specs=out_specs,
        out_shape=out_shape,
        scratch_shapes=[pltpu.VMEM((tm, D), BF16)],
        compiler_params=_params(("parallel", "arbitrary"), 56),
        name=name,
    )(*args)
    return outs if w_extra is not None else outs[0]


def _outproj_body(a1_ref, a2_ref, w1_ref, w2_ref, r_ref, g_ref, x_ref, h_ref):
    x = r_ref[...] + _dot(a1_ref[...], w1_ref[...]) + _dot(a2_ref[...], w2_ref[...])
    x_ref[...] = x
    ms = jnp.mean(x * x, axis=-1, keepdims=True)
    h_ref[...] = (x * lax.rsqrt(ms + EPS) * g_ref[...]).astype(h_ref.dtype)


def out_projection(a1, a2, w, res, g):
    T, K1 = a1.shape
    K2 = a2.shape[1]
    D = w.shape[1]
    assert K1 == K2 and w.shape[0] == K1 + K2
    tm = min(T, 512)
    return pl.pallas_call(
        _outproj_body,
        grid=(T // tm,),
        in_specs=[
            pl.BlockSpec((tm, K1), lambda i: (i, 0)),
            pl.BlockSpec((tm, K2), lambda i: (i, 0)),
            pl.BlockSpec((K1, D), lambda i: (0, 0)),
            pl.BlockSpec((K2, D), lambda i: (1, 0)),
            pl.BlockSpec((tm, D), lambda i: (i, 0)),
            pl.BlockSpec((1, D), lambda i: (0, 0)),
        ],
        out_specs=[pl.BlockSpec((tm, D), lambda i: (i, 0)), pl.BlockSpec((tm, D), lambda i: (i, 0))],
        out_shape=[jax.ShapeDtypeStruct((T, D), F32), jax.ShapeDtypeStruct((T, D), BF16)],
        compiler_params=_params(("parallel",), 56),
        name="out_projection",
    )(a1, a2, w, w, res, g.reshape(1, D).astype(F32))


HALO_ROWS = 16


def _outproj_conv_body(a1_ref, gb_ref, gc_ref, hin_ref, gch_ref, hinh_ref, cw_ref, w1_ref, w2_ref, r_ref, g_ref,
                       x_ref, h_ref, *, tiles_per_seq):
    first = pl.program_id(0) % tiles_per_seq == 0
    halo = gch_ref[...].astype(F32) * hinh_ref[...].astype(F32)
    halo = jnp.where(first, 0.0, halo)
    xe = jnp.concatenate([halo, gc_ref[...].astype(F32) * hin_ref[...].astype(F32)], axis=0)
    cw = cw_ref[...]
    conv = (pltpu.roll(xe, 2, 0) * cw[0:1, :] + pltpu.roll(xe, 1, 0) * cw[1:2, :] + xe * cw[2:3, :])[HALO_ROWS:, :]
    a2 = (gb_ref[...].astype(F32) * conv).astype(BF16)
    x = r_ref[...] + _dot(a1_ref[...], w1_ref[...]) + _dot(a2, w2_ref[...])
    x_ref[...] = x
    ms = jnp.mean(x * x, axis=-1, keepdims=True)
    h_ref[...] = (x * lax.rsqrt(ms + EPS) * g_ref[...]).astype(h_ref.dtype)


def out_projection_conv(a1, p, conv_w, w, res, g, *, seq, col_block):
    T, K = a1.shape
    D = w.shape[1]
    assert w.shape[0] == 2 * K
    tm = min(seq, 512)
    assert seq % tm == 0 and tm % HALO_ROWS == 0
    hb = tm // HALO_ROWS

    def halo_rows(i):
        return jnp.maximum(i * hb - 1, 0)

    body = functools.partial(_outproj_conv_body, tiles_per_seq=seq // tm)
    return pl.pallas_call(
        body,
        grid=(T // tm,),
        in_specs=[
            pl.BlockSpec((tm, K), lambda i: (i, 0)),
            pl.BlockSpec((tm, K), lambda i: (i, col_block)),
            pl.BlockSpec((tm, K), lambda i: (i, col_block + 1)),
            pl.BlockSpec((tm, K), lambda i: (i, col_block + 2)),
            pl.BlockSpec((HALO_ROWS, K), lambda i: (halo_rows(i), col_block + 1)),
            pl.BlockSpec((HALO_ROWS, K), lambda i: (halo_rows(i), col_block + 2)),
            pl.BlockSpec((3, K), lambda i: (0, 0)),
            pl.BlockSpec((K, D), lambda i: (0, 0)),
            pl.BlockSpec((K, D), lambda i: (1, 0)),
            pl.BlockSpec((tm, D), lambda i: (i, 0)),
            pl.BlockSpec((1, D), lambda i: (0, 0)),
        ],
        out_specs=[pl.BlockSpec((tm, D), lambda i: (i, 0)), pl.BlockSpec((tm, D), lambda i: (i, 0))],
        out_shape=[jax.ShapeDtypeStruct((T, D), F32), jax.ShapeDtypeStruct((T, D), BF16)],
        compiler_params=_params(("parallel",), 56),
        name="out_projection_conv",
    )(a1, p, p, p, p, p, conv_w, w, w, res, g.reshape(1, D).astype(F32))


FFN_PIECE = 128


def _ffn_up_body(h_ref, wg_ref, wu_ref, cg_ref, cu_ref, o_ref, *, row_chunk):
    rows = h_ref.shape[0]
    pieces = [slice(c0, c0 + FFN_PIECE) for c0 in range(0, o_ref.shape[1], FFN_PIECE)]
    w = jnp.concatenate([ref[:, cs].astype(BF16) for cs in pieces for ref in (wg_ref, wu_ref)], axis=1)
    cw = jnp.concatenate([ref[:, cs] for cs in pieces for ref in (cg_ref, cu_ref)], axis=1)
    a = jnp.concatenate([_dot(h_ref[r:r + row_chunk, :], w) for r in range(0, rows, row_chunk)], axis=0)
    y = _causal_conv3(a, cw)
    for k, cs in enumerate(pieces):
        gate = y[:, 2 * k * FFN_PIECE:(2 * k + 1) * FFN_PIECE]
        up = y[:, (2 * k + 1) * FFN_PIECE:(2 * k + 2) * FFN_PIECE]
        o_ref[:, cs] = (gate * (1.0 / (1.0 + jnp.exp(-gate))) * up).astype(o_ref.dtype)


def ffn_up(h, w_up, conv_w, *, seq, tn):
    T, D = h.shape
    F = w_up.shape[1] // 2
    tn = _tile(F, tn)
    assert T % seq == 0 and tn % FFN_PIECE == 0
    nf = F // tn
    body = functools.partial(_ffn_up_body, row_chunk=_tile(seq, 512))
    return pl.pallas_call(
        body,
        grid=(T // seq, nf),
        in_specs=[
            pl.BlockSpec((seq, D), lambda b, j: (b, 0)),
            pl.BlockSpec((D, tn), lambda b, j: (0, j)),
            pl.BlockSpec((D, tn), lambda b, j: (0, nf + j)),
            pl.BlockSpec((3, tn), lambda b, j: (0, j)),
            pl.BlockSpec((3, tn), lambda b, j: (0, nf + j)),
        ],
        out_specs=pl.BlockSpec((seq, tn), lambda b, j: (b, j)),
        out_shape=jax.ShapeDtypeStruct((T, F), BF16),
        compiler_params=_params(("parallel", "arbitrary"), 56),
        name="ffn_up",
    )(h, w_up, w_up, conv_w, conv_w)


def _short_conv_body(gb_ref, gc_ref, hin_ref, cw_ref, o_ref):
    x = gc_ref[...].astype(F32) * hin_ref[...].astype(F32)
    o_ref[...] = (gb_ref[...].astype(F32) * _causal_conv3(x, cw_ref[...])).astype(o_ref.dtype)


def gated_short_conv(p, conv_w, *, seq, col0, width):
    T = p.shape[0]
    tc = min(width, 256)
    assert col0 % tc == 0 and width % tc == 0
    c0, nw = col0 // tc, width // tc
    return pl.pallas_call(
        _short_conv_body,
        grid=(T // seq, nw),
        in_specs=[
            pl.BlockSpec((seq, tc), lambda b, j: (b, c0 + j)),
            pl.BlockSpec((seq, tc), lambda b, j: (b, c0 + nw + j)),
            pl.BlockSpec((seq, tc), lambda b, j: (b, c0 + 2 * nw + j)),
            pl.BlockSpec((3, tc), lambda b, j: (0, j)),
        ],
        out_specs=pl.BlockSpec((seq, tc), lambda b, j: (b, j)),
        out_shape=jax.ShapeDtypeStruct((T, width), BF16),
        compiler_params=_params(("parallel", "parallel"), 32),
        name="gated_short_conv",
    )(p, p, p, conv_w)


def _sb_attention_body(q_ref, k_ref, v_ref, o_ref, acc_ref, r_ref, *, n_heads, scale):
    blk = ATT_BLOCK
    i = pl.program_id(1)
    row = lax.broadcasted_iota(jnp.int32, (blk, blk), 0)
    col = lax.broadcasted_iota(jnp.int32, (blk, blk), 1)
    strict = col < row
    suffix = jnp.where(row >= col, 1.0, 0.0).astype(BF16)

    def block(j, diagonal):
        start = pl.multiple_of(j * blk, blk)
        heads = [slice(h * HEAD_DIM, (h + 1) * HEAD_DIM) for h in range(n_heads)]
        zs = [_dot_nt(q_ref[:, hs], k_ref[pl.ds(start, blk), hs]) * scale for hs in heads]
        css = []
        for z in zs:
            sp = _softplus(z)
            if diagonal:
                sp = jnp.where(strict, sp, 0.0)
            sp_hi = sp.astype(BF16)
            sp_lo = (sp - sp_hi.astype(F32)).astype(BF16)
            css.append(_dot(sp_hi, suffix) + _dot(sp_lo, suffix))
        r_min = None
        for h, hs in enumerate(heads):
            z, cs = zs[h], css[h]
            vb = v_ref[pl.ds(start, blk), hs]
            if diagonal:
                a = jnp.where(strict, jnp.exp(z - cs), 0.0)
                r = cs[:, 0:1]
                acc_ref[h] = _dot(a.astype(BF16), vb)
            else:
                r = r_ref[h]
                a = jnp.exp(z - (cs + r))
                r = r + cs[:, 0:1]
                acc_ref[h] += _dot(a.astype(BF16), vb)
            r_ref[h] = r
            r_min = r if r_min is None else jnp.minimum(r_min, r)
        return jnp.min(r_min)

    def cond(carry):
        return jnp.logical_and(carry[0] >= 0, carry[1] < DEAD_EXPONENT)

    def body(carry):
        return carry[0] - 1, block(carry[0], False)

    lax.while_loop(cond, body, (i - 1, block(i, True)))
    for h in range(n_heads):
        o_ref[:, h * HEAD_DIM:(h + 1) * HEAD_DIM] = acc_ref[h].astype(o_ref.dtype)


def sb_attention(p, *, seq, n_heads):
    T = p.shape[0]
    W = n_heads * HEAD_DIM
    blk = ATT_BLOCK
    assert seq % blk == 0
    nq = seq // blk
    body = functools.partial(_sb_attention_body, n_heads=n_heads, scale=HEAD_DIM ** -0.5)
    return pl.pallas_call(
        body,
        grid=(T // seq, nq),
        in_specs=[
            pl.BlockSpec((blk, W), lambda b, i: (b * nq + i, 0)),
            pl.BlockSpec((seq, W), lambda b, i: (b, 1)),
            pl.BlockSpec((seq, W), lambda b, i: (b, 2)),
        ],
        out_specs=pl.BlockSpec((blk, W), lambda b, i: (b * nq + i, 0)),
        out_shape=jax.ShapeDtypeStruct((T, W), BF16),
        scratch_shapes=[pltpu.VMEM((n_heads, blk, HEAD_DIM), F32), pltpu.VMEM((n_heads, blk, 1), F32)],
        compiler_params=_params(("parallel", "arbitrary"), 48),
        name="sb_attention",
    )(p, p, p)


def _forget_cumsum_body(f_ref, b_ref, c_ref):
    x = f_ref[0] + b_ref[...]
    lf = -_softplus(-x)
    S = lf.shape[1]
    lane = lax.broadcasted_iota(jnp.int32, lf.shape, 1)
    k = 1
    while k < S:
        lf = lf + jnp.where(lane >= k, pltpu.roll(lf, k, 1), 0.0)
        k *= 2
    c_ref[0] = lf


def forget_cumsum(f, bias):
    B, H, S = f.shape
    return pl.pallas_call(
        _forget_cumsum_body,
        grid=(B,),
        in_specs=[pl.BlockSpec((1, H, S), lambda b: (b, 0, 0)), pl.BlockSpec((H, 1), lambda b: (0, 0))],
        out_specs=pl.BlockSpec((1, H, S), lambda b: (b, 0, 0)),
        out_shape=jax.ShapeDtypeStruct((B, H, S), F32),
        compiler_params=_params(("parallel",), 16),
        name="forget_cumsum",
    )(f, bias.reshape(H, 1).astype(F32))


def _fox_attention_body(q_ref, k_ref, v_ref, c_ref, o_ref, acc_ref, m_ref, kmax_ref, *, n_heads, scale):
    blk = ATT_BLOCK
    i = pl.program_id(1)
    row = lax.broadcasted_iota(jnp.int32, (blk, blk), 0)
    col = lax.broadcasted_iota(jnp.int32, (blk, blk), 1)
    causal = col <= row
    heads = [slice(h * HEAD_DIM, (h + 1) * HEAD_DIM) for h in range(n_heads)]
    ones_blk = jnp.ones((blk, HEAD_DIM), BF16)

    @pl.when(i == 0)
    def _():
        ones = jnp.ones((8, HEAD_DIM), BF16)
        for h, hs in enumerate(heads):
            k = k_ref[:, hs].astype(F32)
            n2 = _dot_nt(ones, (k * k).astype(BF16))[0:1, :]
            kmax_ref[h:h + 1, :] = jnp.broadcast_to(
                jnp.sqrt(jnp.max(n2, axis=-1, keepdims=True)) * NORM_SLACK, (1, HEAD_DIM))

    qk_bound = []
    for h, hs in enumerate(heads):
        q = q_ref[:, hs].astype(F32)
        qn2 = jnp.max(jnp.sum(q * q, axis=-1, keepdims=True), axis=0, keepdims=True)
        qk_bound.append(jnp.sqrt(qn2) * kmax_ref[h:h + 1, 0:1] * scale)

    def block(j, diagonal):
        start = pl.multiple_of(j * blk, blk)
        c_blk = c_ref[0, pl.ds(j, 1)][0]
        reach = None
        scores = [_dot_nt(q_ref[:, hs], k_ref[pl.ds(start, blk), hs]) for hs in heads]
        for h, hs in enumerate(heads):
            vb = jnp.concatenate([v_ref[pl.ds(start, blk), hs], ones_blk], axis=1)
            s = scores[h] * scale - c_blk[h:h + 1, :]
            if diagonal:
                s = jnp.where(causal, s, -jnp.inf)
                m = jnp.max(s, axis=-1, keepdims=True)
                acc_ref[h] = _dot(jnp.exp(s - m).astype(BF16), vb)
            else:
                m_old = m_ref[h]
                m = jnp.maximum(m_old, jnp.max(s, axis=-1, keepdims=True))
                acc_ref[h] = jnp.exp(m_old - m) * acc_ref[h] + _dot(jnp.exp(s - m).astype(BF16), vb)
            m_ref[h] = m
            r = (qk_bound[h] - c_blk[h:h + 1, 0:1]) - m
            reach = r if reach is None else jnp.maximum(reach, r)
        return jnp.max(reach)

    def cond(carry):
        return jnp.logical_and(carry[0] >= 0, carry[1] > -DEAD_EXPONENT)

    def body(carry):
        return carry[0] - 1, block(carry[0], False)

    lax.while_loop(cond, body, (i - 1, block(i, True)))
    for h, hs in enumerate(heads):
        o_ref[:, hs] = (acc_ref[h, :, 0:HEAD_DIM] / acc_ref[h, :, HEAD_DIM:]).astype(o_ref.dtype)


def fox_attention(p, c, *, seq, n_heads, col_block):
    T = p.shape[0]
    B = T // seq
    W = n_heads * HEAD_DIM
    blk = ATT_BLOCK
    nq = seq // blk
    c_blocks = c.reshape(B, n_heads, nq, blk).transpose(0, 2, 1, 3)
    body = functools.partial(_fox_attention_body, n_heads=n_heads, scale=HEAD_DIM ** -0.5)
    return pl.pallas_call(
        body,
        grid=(B, nq),
        in_specs=[
            pl.BlockSpec((blk, W), lambda b, i: (b * nq + i, col_block)),
            pl.BlockSpec((seq, W), lambda b, i: (b, col_block + 1)),
            pl.BlockSpec((seq, W), lambda b, i: (b, col_block + 2)),
            pl.BlockSpec((1, nq, n_heads, blk), lambda b, i: (b, 0, 0, 0)),
        ],
        out_specs=pl.BlockSpec((blk, W), lambda b, i: (b * nq + i, 0)),
        out_shape=jax.ShapeDtypeStruct((T, W), BF16),
        scratch_shapes=[
            pltpu.VMEM((n_heads, blk, 2 * HEAD_DIM), F32),
            pltpu.VMEM((n_heads, blk, 1), F32),
            pltpu.VMEM((n_heads, HEAD_DIM), F32),
        ],
        compiler_params=_params(("arbitrary", "arbitrary"), 48),
        name="fox_attention",
    )(p, p, p, c_blocks)


def _spatial_gate_body(u_ref, v_ref, w_ref, bt_ref, g_ref, o_ref, *, n_groups):
    tm = u_ref.shape[0]
    v = _gelu_tanh(v_ref[...].astype(F32))
    mu = jnp.mean(v, axis=-1, keepdims=True)
    vc = v - mu
    var = jnp.mean(vc * vc, axis=-1, keepdims=True)
    vn = (vc * lax.rsqrt(var + EPS) * g_ref[...]).astype(BF16)
    row = lax.broadcasted_iota(jnp.int32, (CHUNK, CHUNK), 0)
    col = lax.broadcasted_iota(jnp.int32, (CHUNK, CHUNK), 1)
    for g in range(n_groups):
        gs = slice(g * SG_GROUP_DIM, (g + 1) * SG_GROUP_DIM)
        w = jnp.where(col <= row, w_ref[g], 0.0).astype(BF16)
        b = bt_ref[:, g:g + 1]
        for c in range(tm // CHUNK):
            cs = slice(c * CHUNK, (c + 1) * CHUNK)
            mixed = _dot(w, vn[cs, gs]) + b
            o_ref[cs, gs] = (_gelu_tanh(u_ref[cs, gs].astype(F32)) * mixed).astype(o_ref.dtype)


def spatial_gate(p, sg_w, sg_b, sg_g, *, width):
    T = p.shape[0]
    G = width // SG_GROUP_DIM
    tm = min(T, 512)
    assert tm % CHUNK == 0
    body = functools.partial(_spatial_gate_body, n_groups=G)
    return pl.pallas_call(
        body,
        grid=(T // tm,),
        in_specs=[
            pl.BlockSpec((tm, width), lambda i: (i, 0)),
            pl.BlockSpec((tm, width), lambda i: (i, 1)),
            pl.BlockSpec((G, CHUNK, CHUNK), lambda i: (0, 0, 0)),
            pl.BlockSpec((CHUNK, G), lambda i: (0, 0)),
            pl.BlockSpec((1, width), lambda i: (0, 0)),
        ],
        out_specs=pl.BlockSpec((tm, width), lambda i: (i, 0)),
        out_shape=jax.ShapeDtypeStruct((T, width), BF16),
        compiler_params=_params(("parallel",), 32),
        name="spatial_gate",
    )(p, p, sg_w, sg_b.T, sg_g.reshape(1, width).astype(F32))


def _conv_ffn(x, h, w_up, conv_w, w_down, seq):
    g = ffn_up(h, w_up, conv_w, seq=seq, tn=512)
    return matmul(g, w_down.astype(BF16), x, tm=1024, tn=512, out_dtype=F32, name="ffn_down")


def kernel(x, l0_mix_norm_g, l0_w_in, l0_sc_conv_w, l0_w_out, l0_ffn_norm_g, l0_ffn_up, l0_ffn_conv_w, l0_ffn_down, l1_mix_norm_g, l1_w_in, l1_fox_b_f, l1_sg_w, l1_sg_b, l1_sg_norm_g, l1_w_out, l1_ffn_norm_g, l1_ffn_up, l1_ffn_conv_w, l1_ffn_down, final_norm_g):
    B, S, D = x.shape
    T = B * S
    half = D // 2
    n_heads = half // HEAD_DIM
    x = x.reshape(T, D)

    p = in_projection(x, l0_mix_norm_g, l0_w_in, n_cols=6 * half, tm=1024, tn=1024, name="in_projection0")
    a_out = sb_attention(p, seq=S, n_heads=n_heads)
    x, h = out_projection_conv(a_out, p, l0_sc_conv_w, l0_w_out.astype(BF16), x, l0_ffn_norm_g,
                               seq=S, col_block=3)
    x = _conv_ffn(x, h, l0_ffn_up, l0_ffn_conv_w, l0_ffn_down, S)

    n_main = 5 * half
    w_f = jnp.pad(l1_w_in[:, n_main:], ((0, 0), (0, HEAD_DIM - n_heads))).astype(BF16)
    p, f = in_projection(x, l1_mix_norm_g, l1_w_in, w_f, n_cols=n_main,
                         tm=1024, tn=1024, name="in_projection1")
    f = f[:, :n_heads].reshape(B, S, n_heads).transpose(0, 2, 1)
    c = forget_cumsum(f, l1_fox_b_f)
    c_out = spatial_gate(p, l1_sg_w, l1_sg_b, l1_sg_norm_g, width=half)
    d_out = fox_attention(p, c, seq=S, n_heads=n_heads, col_block=2)
    x, h = out_projection(c_out, d_out, l1_w_out.astype(BF16), x, l1_ffn_norm_g)
    x = _conv_ffn(x, h, l1_ffn_up, l1_ffn_conv_w, l1_ffn_down, S)

    return rmsnorm(x, final_norm_g, F32).reshape(B, S, D)
```

```python
import functools

import jax
import jax.numpy as jnp
from jax import lax
from jax.experimental import pallas as pl
from jax.experimental.pallas import tpu as pltpu

EPS = 1e-6
HEAD_DIM = 128
CHUNK = 128
SG_GROUP_DIM = 128
ATT_BLOCK = 256
DEAD_EXPONENT = 105.0
NORM_SLACK = 1.01
FIXED_CEILING_LIMIT = 60.0

V7X_VMEM_BYTES = 64 * 1024 * 1024
MIB = 1024 * 1024

F32 = jnp.float32
BF16 = jnp.bfloat16


def _params(semantics, vmem_mib, flags=None):
    assert vmem_mib * MIB < V7X_VMEM_BYTES
    return pltpu.CompilerParams(dimension_semantics=semantics, vmem_limit_bytes=vmem_mib * MIB, flags=flags)


def _tile(n, target, quantum=128):
    if n <= target:
        return n
    t = (target // quantum) * quantum
    while n % t:
        t -= quantum
    return t


def _dot(a, b):
    return jnp.dot(a, b, preferred_element_type=F32)


def _dot_nt(a, b):
    return lax.dot_general(a, b, (((1,), (1,)), ((), ())), preferred_element_type=F32)


def _softplus(z):
    return jnp.maximum(z, 0.0) + jnp.log(1.0 + jnp.exp(-jnp.abs(z)))


def _gelu_tanh(x):
    return 0.5 * x * (1.0 + jnp.tanh(0.7978845608028654 * (x + 0.044715 * (x * x * x))))


SUBLANES = 8


def _causal_conv3(a, cw):
    rows, cols = a.shape
    a3 = a.reshape(rows // SUBLANES, SUBLANES, cols)
    sub = lax.broadcasted_iota(jnp.int32, a3.shape, 1)

    def shifted(k):
        rot = pltpu.roll(a3, k, 1)
        above = jnp.concatenate([jnp.zeros((1, SUBLANES, cols), a.dtype), rot[:-1]], axis=0)
        return jnp.where(sub < k, above, rot)

    y = shifted(2) * cw[0:1, :] + shifted(1) * cw[1:2, :] + a3 * cw[2:3, :]
    return y.reshape(rows, cols)


def _rmsnorm_body(x_ref, g_ref, o_ref):
    x = x_ref[...]
    ms = jnp.mean(x * x, axis=-1, keepdims=True)
    o_ref[...] = (x * lax.rsqrt(ms + EPS) * g_ref[...]).astype(o_ref.dtype)


def rmsnorm(x, g, out_dtype):
    T, D = x.shape
    tm = min(T, 512)
    return pl.pallas_call(
        _rmsnorm_body,
        grid=(T // tm,),
        in_specs=[pl.BlockSpec((tm, D), lambda i: (i, 0)), pl.BlockSpec((1, D), lambda i: (0, 0))],
        out_specs=pl.BlockSpec((tm, D), lambda i: (i, 0)),
        out_shape=jax.ShapeDtypeStruct((T, D), out_dtype),
        compiler_params=_params(("parallel",), 40),
        name="rmsnorm",
    )(x, g.reshape(1, D).astype(F32))


def _mm_body(a_ref, w_ref, o_ref):
    o_ref[...] = _dot(a_ref[...], w_ref[...]).astype(o_ref.dtype)


def _mm_res_body(a_ref, w_ref, r_ref, o_ref):
    o_ref[...] = r_ref[...] + _dot(a_ref[...], w_ref[...])


def matmul(a, w, res=None, *, tm, tn, out_dtype, name):
    M, K = a.shape
    N = w.shape[1]
    tm, tn = _tile(M, tm), _tile(N, tn)
    in_specs = [pl.BlockSpec((tm, K), lambda i, j: (i, 0)), pl.BlockSpec((K, tn), lambda i, j: (0, j))]
    args = [a, w]
    body = _mm_body
    if res is not None:
        in_specs.append(pl.BlockSpec((tm, tn), lambda i, j: (i, j)))
        args.append(res)
        body = _mm_res_body
    return pl.pallas_call(
        body,
        grid=(M // tm, N // tn),
        in_specs=in_specs,
        out_specs=pl.BlockSpec((tm, tn), lambda i, j: (i, j)),
        out_shape=jax.ShapeDtypeStruct((M, N), out_dtype),
        compiler_params=_params(("parallel", "arbitrary"), 56),
        name=name,
    )(*args)


def _mm_res_norm_body(a_ref, w_ref, r_ref, g_ref, o_ref, *, n_tiles, tn):
    n = pl.program_id(1)
    for k in range(n_tiles):
        @pl.when(n == k)
        def _(k=k):
            o_ref[:, k * tn:(k + 1) * tn] = r_ref[...] + _dot(a_ref[...], w_ref[...])

    @pl.when(n == n_tiles - 1)
    def _():
        x = o_ref[...]
        ms = jnp.mean(x * x, axis=-1, keepdims=True)
        o_ref[...] = x * lax.rsqrt(ms + EPS) * g_ref[...]


def matmul_res_rmsnorm(a, w, res, g, *, tm, tn, name):
    M, K = a.shape
    N = w.shape[1]
    tm, tn = _tile(M, tm), _tile(N, tn)
    body = functools.partial(_mm_res_norm_body, n_tiles=N // tn, tn=tn)
    return pl.pallas_call(
        body,
        grid=(M // tm, N // tn),
        in_specs=[
            pl.BlockSpec((tm, K), lambda i, j: (i, 0)),
            pl.BlockSpec((K, tn), lambda i, j: (0, j)),
            pl.BlockSpec((tm, tn), lambda i, j: (i, j)),
            pl.BlockSpec((1, N), lambda i, j: (0, 0)),
        ],
        out_specs=pl.BlockSpec((tm, N), lambda i, j: (i, 0)),
        out_shape=jax.ShapeDtypeStruct((M, N), F32),
        compiler_params=_params(("parallel", "arbitrary"), 56),
        name=name,
    )(a, w, res, g.reshape(1, N).astype(F32))


def _in_proj_body(x_ref, g_ref, w_ref, *rest, with_extra):
    if with_extra:
        we_ref, o_ref, e_ref, h_s = rest
    else:
        o_ref, h_s = rest

    @pl.when(pl.program_id(1) == 0)
    def _():
        x = x_ref[...]
        ms = jnp.mean(x * x, axis=-1, keepdims=True)
        h_s[...] = (x * lax.rsqrt(ms + EPS) * g_ref[...]).astype(BF16)
        if with_extra:
            e_ref[...] = _dot(h_s[...], we_ref[...])

    o_ref[...] = _dot(h_s[...], w_ref[...].astype(BF16)).astype(o_ref.dtype)


def in_projection(x, g, w, w_extra=None, *, n_cols, tm, tn, name):
    T, D = x.shape
    N = n_cols
    tm, tn = _tile(T, tm), _tile(N, tn)
    in_specs = [
        pl.BlockSpec((tm, D), lambda i, j: (i, 0)),
        pl.BlockSpec((1, D), lambda i, j: (0, 0)),
        pl.BlockSpec((D, tn), lambda i, j: (0, j)),
    ]
    args = [x, g.reshape(1, D).astype(F32), w]
    out_specs = [pl.BlockSpec((tm, tn), lambda i, j: (i, j))]
    out_shape = [jax.ShapeDtypeStruct((T, N), BF16)]
    if w_extra is not None:
        ne = w_extra.shape[1]
        in_specs.append(pl.BlockSpec((D, ne), lambda i, j: (0, 0)))
        args.append(w_extra)
        out_specs.append(pl.BlockSpec((tm, ne), lambda i, j: (i, 0)))
        out_shape.append(jax.ShapeDtypeStruct((T, ne), F32))
    outs = pl.pallas_call(
        functools.partial(_in_proj_body, with_extra=w_extra is not None),
        grid=(T // tm, N // tn),
        in_specs=in_specs,
        out_specs=out_specs,
        out_shape=out_shape,
        scratch_shapes=[pltpu.VMEM((tm, D), BF16)],
        compiler_params=_params(("parallel", "arbitrary"), 56),
        name=name,
    )(*args)
    return outs if w_extra is not None else outs[0]


def _outproj_body(a1_ref, a2_ref, w1_ref, w2_ref, r_ref, g_ref, x_ref, h_ref):
    x = r_ref[...] + _dot(a1_ref[...], w1_ref[...]) + _dot(a2_ref[...], w2_ref[...])
    x_ref[...] = x
    ms = jnp.mean(x * x, axis=-1, keepdims=True)
    h_ref[...] = (x * lax.rsqrt(ms + EPS) * g_ref[...]).astype(h_ref.dtype)


def out_projection(a1, a2, w, res, g):
    T, K1 = a1.shape
    K2 = a2.shape[1]
    D = w.shape[1]
    assert K1 == K2 and w.shape[0] == K1 + K2
    tm = min(T, 512)
    return pl.pallas_call(
        _outproj_body,
        grid=(T // tm,),
        in_specs=[
            pl.BlockSpec((tm, K1), lambda i: (i, 0)),
            pl.BlockSpec((tm, K2), lambda i: (i, 0)),
            pl.BlockSpec((K1, D), lambda i: (0, 0)),
            pl.BlockSpec((K2, D), lambda i: (1, 0)),
            pl.BlockSpec((tm, D), lambda i: (i, 0)),
            pl.BlockSpec((1, D), lambda i: (0, 0)),
        ],
        out_specs=[pl.BlockSpec((tm, D), lambda i: (i, 0)), pl.BlockSpec((tm, D), lambda i: (i, 0))],
        out_shape=[jax.ShapeDtypeStruct((T, D), F32), jax.ShapeDtypeStruct((T, D), BF16)],
        compiler_params=_params(("parallel",), 56),
        name="out_projection",
    )(a1, a2, w, w, res, g.reshape(1, D).astype(F32))


HALO_ROWS = 16


def _outproj_conv_body(a1_ref, gb_ref, gc_ref, hin_ref, gch_ref, hinh_ref, cw_ref, w1_ref, w2_ref, r_ref, g_ref,
                       x_ref, h_ref, *, tiles_per_seq):
    first = pl.program_id(0) % tiles_per_seq == 0
    halo = gch_ref[...].astype(F32) * hinh_ref[...].astype(F32)
    halo = jnp.where(first, 0.0, halo)
    xe = jnp.concatenate([halo, gc_ref[...].astype(F32) * hin_ref[...].astype(F32)], axis=0)
    conv = _causal_conv3(xe, cw_ref[...])[HALO_ROWS:, :]
    a2 = (gb_ref[...].astype(F32) * conv).astype(BF16)
    x = r_ref[...] + _dot(a1_ref[...], w1_ref[...]) + _dot(a2, w2_ref[...])
    x_ref[...] = x
    ms = jnp.mean(x * x, axis=-1, keepdims=True)
    h_ref[...] = (x * lax.rsqrt(ms + EPS) * g_ref[...]).astype(h_ref.dtype)


def out_projection_conv(a1, p, conv_w, w, res, g, *, seq, col_block):
    T, K = a1.shape
    D = w.shape[1]
    assert w.shape[0] == 2 * K
    tm = min(seq, 512)
    assert seq % tm == 0 and tm % HALO_ROWS == 0
    hb = tm // HALO_ROWS

    def halo_rows(i):
        return jnp.maximum(i * hb - 1, 0)

    body = functools.partial(_outproj_conv_body, tiles_per_seq=seq // tm)
    return pl.pallas_call(
        body,
        grid=(T // tm,),
        in_specs=[
            pl.BlockSpec((tm, K), lambda i: (i, 0)),
            pl.BlockSpec((tm, K), lambda i: (i, col_block)),
            pl.BlockSpec((tm, K), lambda i: (i, col_block + 1)),
            pl.BlockSpec((tm, K), lambda i: (i, col_block + 2)),
            pl.BlockSpec((HALO_ROWS, K), lambda i: (halo_rows(i), col_block + 1)),
            pl.BlockSpec((HALO_ROWS, K), lambda i: (halo_rows(i), col_block + 2)),
            pl.BlockSpec((3, K), lambda i: (0, 0)),
            pl.BlockSpec((K, D), lambda i: (0, 0)),
            pl.BlockSpec((K, D), lambda i: (1, 0)),
            pl.BlockSpec((tm, D), lambda i: (i, 0)),
            pl.BlockSpec((1, D), lambda i: (0, 0)),
        ],
        out_specs=[pl.BlockSpec((tm, D), lambda i: (i, 0)), pl.BlockSpec((tm, D), lambda i: (i, 0))],
        out_shape=[jax.ShapeDtypeStruct((T, D), F32), jax.ShapeDtypeStruct((T, D), BF16)],
        compiler_params=_params(("parallel",), 56),
        name="out_projection_conv",
    )(a1, p, p, p, p, p, conv_w, w, w, res, g.reshape(1, D).astype(F32))


FFN_PIECE = 128


def _ffn_up_body(h_ref, wg_ref, wu_ref, cg_ref, cu_ref, o_ref, *, row_chunk):
    rows = h_ref.shape[0]
    pieces = [slice(c0, c0 + FFN_PIECE) for c0 in range(0, o_ref.shape[1], FFN_PIECE)]
    w = jnp.concatenate([ref[:, cs].astype(BF16) for cs in pieces for ref in (wg_ref, wu_ref)], axis=1)
    cw = jnp.concatenate([ref[:, cs] for cs in pieces for ref in (cg_ref, cu_ref)], axis=1)
    a = jnp.concatenate([_dot(h_ref[r:r + row_chunk, :], w) for r in range(0, rows, row_chunk)], axis=0)
    y = _causal_conv3(a, cw)
    for k, cs in enumerate(pieces):
        gate = y[:, 2 * k * FFN_PIECE:(2 * k + 1) * FFN_PIECE]
        up = y[:, (2 * k + 1) * FFN_PIECE:(2 * k + 2) * FFN_PIECE]
        o_ref[:, cs] = (gate * (1.0 / (1.0 + jnp.exp(-gate))) * up).astype(o_ref.dtype)


def ffn_up(h, w_up, conv_w, *, seq, tn):
    T, D = h.shape
    F = w_up.shape[1] // 2
    tn = _tile(F, tn)
    assert T % seq == 0 and tn % FFN_PIECE == 0
    nf = F // tn
    body = functools.partial(_ffn_up_body, row_chunk=_tile(seq, 512))
    return pl.pallas_call(
        body,
        grid=(T // seq, nf),
        in_specs=[
            pl.BlockSpec((seq, D), lambda b, j: (b, 0)),
            pl.BlockSpec((D, tn), lambda b, j: (0, j)),
            pl.BlockSpec((D, tn), lambda b, j: (0, nf + j)),
            pl.BlockSpec((3, tn), lambda b, j: (0, j)),
            pl.BlockSpec((3, tn), lambda b, j: (0, nf + j)),
        ],
        out_specs=pl.BlockSpec((seq, tn), lambda b, j: (b, j)),
        out_shape=jax.ShapeDtypeStruct((T, F), BF16),
        compiler_params=_params(("parallel", "arbitrary"), 56),
        name="ffn_up",
    )(h, w_up, w_up, conv_w, conv_w)


def _short_conv_body(gb_ref, gc_ref, hin_ref, cw_ref, o_ref):
    x = gc_ref[...].astype(F32) * hin_ref[...].astype(F32)
    o_ref[...] = (gb_ref[...].astype(F32) * _causal_conv3(x, cw_ref[...])).astype(o_ref.dtype)


def gated_short_conv(p, conv_w, *, seq, col0, width):
    T = p.shape[0]
    tc = min(width, 256)
    assert col0 % tc == 0 and width % tc == 0
    c0, nw = col0 // tc, width // tc
    return pl.pallas_call(
        _short_conv_body,
        grid=(T // seq, nw),
        in_specs=[
            pl.BlockSpec((seq, tc), lambda b, j: (b, c0 + j)),
            pl.BlockSpec((seq, tc), lambda b, j: (b, c0 + nw + j)),
            pl.BlockSpec((seq, tc), lambda b, j: (b, c0 + 2 * nw + j)),
            pl.BlockSpec((3, tc), lambda b, j: (0, j)),
        ],
        out_specs=pl.BlockSpec((seq, tc), lambda b, j: (b, j)),
        out_shape=jax.ShapeDtypeStruct((T, width), BF16),
        compiler_params=_params(("parallel", "parallel"), 32),
        name="gated_short_conv",
    )(p, p, p, conv_w)


def _sb_attention_body(q_ref, k_ref, v_ref, o_ref, acc_ref, r_ref, *, n_heads, scale):
    blk = ATT_BLOCK
    i = pl.program_id(1)
    row = lax.broadcasted_iota(jnp.int32, (blk, blk), 0)
    col = lax.broadcasted_iota(jnp.int32, (blk, blk), 1)
    strict = col < row
    suffix = jnp.where(row >= col, 1.0, 0.0).astype(BF16)

    def block(j, diagonal):
        start = pl.multiple_of(j * blk, blk)
        heads = [slice(h * HEAD_DIM, (h + 1) * HEAD_DIM) for h in range(n_heads)]
        zs = [_dot_nt(q_ref[:, hs], k_ref[pl.ds(start, blk), hs]) * scale for hs in heads]
        css = []
        for z in zs:
            sp = _softplus(z)
            if diagonal:
                sp = jnp.where(strict, sp, 0.0)
            sp_hi = sp.astype(BF16)
            sp_lo = (sp - sp_hi.astype(F32)).astype(BF16)
            css.append(_dot(sp_hi, suffix) + _dot(sp_lo, suffix))
        r_min = None
        for h, hs in enumerate(heads):
            z, cs = zs[h], css[h]
            vb = v_ref[pl.ds(start, blk), hs]
            if diagonal:
                a = jnp.where(strict, jnp.exp(z - cs), 0.0)
                r = cs[:, 0:1]
                acc_ref[h] = _dot(a.astype(BF16), vb)
            else:
                r = r_ref[h]
                a = jnp.exp(z - (cs + r))
                r = r + cs[:, 0:1]
                acc_ref[h] += _dot(a.astype(BF16), vb)
            r_ref[h] = r
            r_min = r if r_min is None else jnp.minimum(r_min, r)
        return jnp.min(r_min)

    def cond(carry):
        return jnp.logical_and(carry[0] >= 0, carry[1] < DEAD_EXPONENT)

    def body(carry):
        return carry[0] - 1, block(carry[0], False)

    lax.while_loop(cond, body, (i - 1, block(i, True)))
    for h in range(n_heads):
        o_ref[:, h * HEAD_DIM:(h + 1) * HEAD_DIM] = acc_ref[h].astype(o_ref.dtype)


def sb_attention(p, *, seq, n_heads):
    T = p.shape[0]
    W = n_heads * HEAD_DIM
    blk = ATT_BLOCK
    assert seq % blk == 0
    nq = seq // blk
    body = functools.partial(_sb_attention_body, n_heads=n_heads, scale=HEAD_DIM ** -0.5)
    return pl.pallas_call(
        body,
        grid=(T // seq, nq),
        in_specs=[
            pl.BlockSpec((blk, W), lambda b, i: (b * nq + i, 0)),
            pl.BlockSpec((seq, W), lambda b, i: (b, 1)),
            pl.BlockSpec((seq, W), lambda b, i: (b, 2)),
        ],
        out_specs=pl.BlockSpec((blk, W), lambda b, i: (b * nq + i, 0)),
        out_shape=jax.ShapeDtypeStruct((T, W), BF16),
        scratch_shapes=[pltpu.VMEM((n_heads, blk, HEAD_DIM), F32), pltpu.VMEM((n_heads, blk, 1), F32)],
        compiler_params=_params(("parallel", "arbitrary"), 48),
        name="sb_attention",
    )(p, p, p)


def _forget_cumsum_body(f_ref, b_ref, c_ref):
    x = f_ref[0] + b_ref[...]
    lf = -_softplus(-x)
    S = lf.shape[1]
    lane = lax.broadcasted_iota(jnp.int32, lf.shape, 1)
    k = 1
    while k < S:
        lf = lf + jnp.where(lane >= k, pltpu.roll(lf, k, 1), 0.0)
        k *= 2
    c_ref[0] = lf


def forget_cumsum(f, bias):
    B, H, S = f.shape
    return pl.pallas_call(
        _forget_cumsum_body,
        grid=(B,),
        in_specs=[pl.BlockSpec((1, H, S), lambda b: (b, 0, 0)), pl.BlockSpec((H, 1), lambda b: (0, 0))],
        out_specs=pl.BlockSpec((1, H, S), lambda b: (b, 0, 0)),
        out_shape=jax.ShapeDtypeStruct((B, H, S), F32),
        compiler_params=_params(("parallel",), 16),
        name="forget_cumsum",
    )(f, bias.reshape(H, 1).astype(F32))


def _fox_attention_body(q_ref, k_ref, v_ref, c_ref, ccol_ref, o_ref, acc_ref, m_ref, kmax_ref, *, n_heads, scale):
    blk = ATT_BLOCK
    i = pl.program_id(1)
    row = lax.broadcasted_iota(jnp.int32, (blk, blk), 0)
    col = lax.broadcasted_iota(jnp.int32, (blk, blk), 1)
    causal = col <= row
    heads = [slice(h * HEAD_DIM, (h + 1) * HEAD_DIM) for h in range(n_heads)]
    ones_blk = jnp.ones((blk, HEAD_DIM), BF16)

    @pl.when(i == 0)
    def _():
        ones = jnp.ones((8, HEAD_DIM), BF16)
        for h, hs in enumerate(heads):
            k = k_ref[:, hs].astype(F32)
            n2 = _dot_nt(ones, (k * k).astype(BF16))[0:1, :]
            kmax_ref[h:h + 1, :] = jnp.broadcast_to(
                jnp.sqrt(jnp.max(n2, axis=-1, keepdims=True)) * NORM_SLACK, (1, HEAD_DIM))

    qk_bound = []
    for h, hs in enumerate(heads):
        q = q_ref[:, hs].astype(F32)
        qn2 = jnp.max(jnp.sum(q * q, axis=-1, keepdims=True), axis=0, keepdims=True)
        qk_bound.append(jnp.sqrt(qn2) * kmax_ref[h:h + 1, 0:1] * scale)

    def block(j, diagonal, fixed):
        start = pl.multiple_of(j * blk, blk)
        c_blk = c_ref[0, pl.ds(j, 1)][0]
        reach = None
        scores = [_dot_nt(q_ref[:, hs], k_ref[pl.ds(start, blk), hs]) for hs in heads]
        for h, hs in enumerate(heads):
            vb = jnp.concatenate([v_ref[pl.ds(start, blk), hs], ones_blk], axis=1)
            s = scores[h] * scale - c_blk[h:h + 1, :]
            if diagonal:
                s = jnp.where(causal, s, -jnp.inf)
            if fixed:
                c_t = ccol_ref[:, h:h + 1]
                pv = _dot(jnp.exp(s + (c_t - qk_bound[h])).astype(BF16), vb)
                acc_ref[h] = pv if diagonal else acc_ref[h] + pv
                r = (2.0 * qk_bound[h] - c_blk[h:h + 1, 0:1]) + c_t[0:1, :]
            else:
                if diagonal:
                    m = jnp.max(s, axis=-1, keepdims=True)
                    acc_ref[h] = _dot(jnp.exp(s - m).astype(BF16), vb)
                else:
                    m_old = m_ref[h]
                    m = jnp.maximum(m_old, jnp.max(s, axis=-1, keepdims=True))
                    acc_ref[h] = jnp.exp(m_old - m) * acc_ref[h] + _dot(jnp.exp(s - m).astype(BF16), vb)
                m_ref[h] = m
                r = (qk_bound[h] - c_blk[h:h + 1, 0:1]) - m
            reach = r if reach is None else jnp.maximum(reach, r)
        return jnp.max(reach)

    def cond(carry):
        return jnp.logical_and(carry[0] >= 0, carry[1] > -DEAD_EXPONENT)

    def sweep(fixed):
        def body(carry):
            return carry[0] - 1, block(carry[0], False, fixed)

        lax.while_loop(cond, body, (i - 1, block(i, True, fixed)))

    small_logits = 2.0 * jnp.max(jnp.concatenate(qk_bound, axis=1)) <= FIXED_CEILING_LIMIT

    @pl.when(small_logits)
    def _():
        sweep(True)

    @pl.when(jnp.logical_not(small_logits))
    def _():
        sweep(False)

    for h, hs in enumerate(heads):
        o_ref[:, hs] = (acc_ref[h, :, 0:HEAD_DIM] / acc_ref[h, :, HEAD_DIM:]).astype(o_ref.dtype)


def fox_attention(p, c, *, seq, n_heads, col_block):
    T = p.shape[0]
    B = T // seq
    W = n_heads * HEAD_DIM
    blk = ATT_BLOCK
    nq = seq // blk
    c_blocks = c.reshape(B, n_heads, nq, blk).transpose(0, 2, 1, 3)
    c_col = c.transpose(0, 2, 1).reshape(T, n_heads)
    body = functools.partial(_fox_attention_body, n_heads=n_heads, scale=HEAD_DIM ** -0.5)
    return pl.pallas_call(
        body,
        grid=(B, nq),
        in_specs=[
            pl.BlockSpec((blk, W), lambda b, i: (b * nq + i, col_block)),
            pl.BlockSpec((seq, W), lambda b, i: (b, col_block + 1)),
            pl.BlockSpec((seq, W), lambda b, i: (b, col_block + 2)),
            pl.BlockSpec((1, nq, n_heads, blk), lambda b, i: (b, 0, 0, 0)),
            pl.BlockSpec((blk, n_heads), lambda b, i: (b * nq + i, 0)),
        ],
        out_specs=pl.BlockSpec((blk, W), lambda b, i: (b * nq + i, 0)),
        out_shape=jax.ShapeDtypeStruct((T, W), BF16),
        scratch_shapes=[
            pltpu.VMEM((n_heads, blk, 2 * HEAD_DIM), F32),
            pltpu.VMEM((n_heads, blk, 1), F32),
            pltpu.VMEM((n_heads, HEAD_DIM), F32),
        ],
        compiler_params=_params(("arbitrary", "arbitrary"), 48),
        name="fox_attention",
    )(p, p, p, c_blocks, c_col)


def _spatial_gate_body(u_ref, v_ref, w_ref, bt_ref, g_ref, o_ref, *, n_groups):
    tm = u_ref.shape[0]
    v = _gelu_tanh(v_ref[...].astype(F32))
    mu = jnp.mean(v, axis=-1, keepdims=True)
    vc = v - mu
    var = jnp.mean(vc * vc, axis=-1, keepdims=True)
    vn = (vc * lax.rsqrt(var + EPS) * g_ref[...]).astype(BF16)
    row = lax.broadcasted_iota(jnp.int32, (CHUNK, CHUNK), 0)
    col = lax.broadcasted_iota(jnp.int32, (CHUNK, CHUNK), 1)
    for g in range(n_groups):
        gs = slice(g * SG_GROUP_DIM, (g + 1) * SG_GROUP_DIM)
        w = jnp.where(col <= row, w_ref[g], 0.0).astype(BF16)
        b = bt_ref[:, g:g + 1]
        for c in range(tm // CHUNK):
            cs = slice(c * CHUNK, (c + 1) * CHUNK)
            mixed = _dot(w, vn[cs, gs]) + b
            o_ref[cs, gs] = (_gelu_tanh(u_ref[cs, gs].astype(F32)) * mixed).astype(o_ref.dtype)


def spatial_gate(p, sg_w, sg_b, sg_g, *, width):
    T = p.shape[0]
    G = width // SG_GROUP_DIM
    tm = min(T, 512)
    assert tm % CHUNK == 0
    body = functools.partial(_spatial_gate_body, n_groups=G)
    return pl.pallas_call(
        body,
        grid=(T // tm,),
        in_specs=[
            pl.BlockSpec((tm, width), lambda i: (i, 0)),
            pl.BlockSpec((tm, width), lambda i: (i, 1)),
            pl.BlockSpec((G, CHUNK, CHUNK), lambda i: (0, 0, 0)),
            pl.BlockSpec((CHUNK, G), lambda i: (0, 0)),
            pl.BlockSpec((1, width), lambda i: (0, 0)),
        ],
        out_specs=pl.BlockSpec((tm, width), lambda i: (i, 0)),
        out_shape=jax.ShapeDtypeStruct((T, width), BF16),
        compiler_params=_params(("parallel",), 32),
        name="spatial_gate",
    )(p, p, sg_w, sg_b.T, sg_g.reshape(1, width).astype(F32))


def _conv_ffn(x, h, w_up, conv_w, w_down, seq, final_g=None):
    g = ffn_up(h, w_up, conv_w, seq=seq, tn=512)
    if final_g is None:
        return matmul(g, w_down.astype(BF16), x, tm=1024, tn=512, out_dtype=F32, name="ffn_down")
    return matmul_res_rmsnorm(g, w_down.astype(BF16), x, final_g, tm=512, tn=512, name="ffn_down_norm")


def kernel(x, l0_mix_norm_g, l0_w_in, l0_sc_conv_w, l0_w_out, l0_ffn_norm_g, l0_ffn_up, l0_ffn_conv_w, l0_ffn_down, l1_mix_norm_g, l1_w_in, l1_fox_b_f, l1_sg_w, l1_sg_b, l1_sg_norm_g, l1_w_out, l1_ffn_norm_g, l1_ffn_up, l1_ffn_conv_w, l1_ffn_down, final_norm_g):
    B, S, D = x.shape
    T = B * S
    half = D // 2
    n_heads = half // HEAD_DIM
    x = x.reshape(T, D)

    p = in_projection(x, l0_mix_norm_g, l0_w_in, n_cols=6 * half, tm=1024, tn=1024, name="in_projection0")
    a_out = sb_attention(p, seq=S, n_heads=n_heads)
    x, h = out_projection_conv(a_out, p, l0_sc_conv_w, l0_w_out.astype(BF16), x, l0_ffn_norm_g,
                               seq=S, col_block=3)
    x = _conv_ffn(x, h, l0_ffn_up, l0_ffn_conv_w, l0_ffn_down, S)

    n_main = 5 * half
    w_f = jnp.pad(l1_w_in[:, n_main:], ((0, 0), (0, HEAD_DIM - n_heads))).astype(BF16)
    p, f = in_projection(x, l1_mix_norm_g, l1_w_in[:, :n_main].astype(BF16), w_f, n_cols=n_main,
                         tm=1024, tn=1024, name="in_projection1")
    f = f[:, :n_heads].reshape(B, S, n_heads).transpose(0, 2, 1)
    c = forget_cumsum(f, l1_fox_b_f)
    c_out = spatial_gate(p, l1_sg_w, l1_sg_b, l1_sg_norm_g, width=half)
    d_out = fox_attention(p, c, seq=S, n_heads=n_heads, col_block=2)
    x, h = out_projection(c_out, d_out, l1_w_out.astype(BF16), x, l1_ffn_norm_g)
    return _conv_ffn(x, h, l1_ffn_up, l1_ffn_conv_w, l1_ffn_down, S, final_norm_g).reshape(B, S, D)
```

```python
import functools

import jax
import jax.numpy as jnp
from jax import lax
from jax.experimental import pallas as pl
from jax.experimental.pallas import tpu as pltpu

EPS = 1e-6
HEAD_DIM = 128
CHUNK = 128
SG_GROUP_DIM = 128
ATT_BLOCK = 256
DEAD_EXPONENT = 105.0
NORM_SLACK = 1.01
FIXED_CEILING_LIMIT = 60.0

V7X_VMEM_BYTES = 64 * 1024 * 1024
MIB = 1024 * 1024

F32 = jnp.float32
BF16 = jnp.bfloat16


def _params(semantics, vmem_mib):
    assert vmem_mib * MIB < V7X_VMEM_BYTES
    return pltpu.CompilerParams(dimension_semantics=semantics, vmem_limit_bytes=vmem_mib * MIB)


def _tile(n, target, quantum=128):
    if n <= target:
        return n
    t = (target // quantum) * quantum
    while n % t:
        t -= quantum
    return t


def _dot(a, b):
    return jnp.dot(a, b, preferred_element_type=F32)


def _dot_nt(a, b):
    return lax.dot_general(a, b, (((1,), (1,)), ((), ())), preferred_element_type=F32)


def _softplus(z):
    return jnp.maximum(z, 0.0) + jnp.log(1.0 + jnp.exp(-jnp.abs(z)))


def _gelu_tanh(x):
    return 0.5 * x * (1.0 + jnp.tanh(0.7978845608028654 * (x + 0.044715 * (x * x * x))))


SUBLANES = 8


def _causal_conv3(a, cw):
    rows, cols = a.shape
    a3 = a.reshape(rows // SUBLANES, SUBLANES, cols)
    sub = lax.broadcasted_iota(jnp.int32, a3.shape, 1)

    def shifted(k):
        rot = pltpu.roll(a3, k, 1)
        above = jnp.concatenate([jnp.zeros((1, SUBLANES, cols), a.dtype), rot[:-1]], axis=0)
        return jnp.where(sub < k, above, rot)

    y = shifted(2) * cw[0:1, :] + shifted(1) * cw[1:2, :] + a3 * cw[2:3, :]
    return y.reshape(rows, cols)


def _mm_res_body(a_ref, w_ref, r_ref, o_ref):
    o_ref[...] = r_ref[...] + _dot(a_ref[...], w_ref[...])


def matmul_res(a, w, res, *, tm, tn, name):
    M, K = a.shape
    N = w.shape[1]
    tm, tn = _tile(M, tm), _tile(N, tn)
    return pl.pallas_call(
        _mm_res_body,
        grid=(M // tm, N // tn),
        in_specs=[
            pl.BlockSpec((tm, K), lambda i, j: (i, 0)),
            pl.BlockSpec((K, tn), lambda i, j: (0, j)),
            pl.BlockSpec((tm, tn), lambda i, j: (i, j)),
        ],
        out_specs=pl.BlockSpec((tm, tn), lambda i, j: (i, j)),
        out_shape=jax.ShapeDtypeStruct((M, N), F32),
        compiler_params=_params(("parallel", "arbitrary"), 56),
        name=name,
    )(a, w, res)


def _mm_res_norm_body(a_ref, w_ref, r_ref, g_ref, o_ref, *, n_tiles, tn):
    n = pl.program_id(1)
    for k in range(n_tiles):
        @pl.when(n == k)
        def _(k=k):
            o_ref[:, k * tn:(k + 1) * tn] = r_ref[...] + _dot(a_ref[...], w_ref[...])

    @pl.when(n == n_tiles - 1)
    def _():
        x = o_ref[...]
        ms = jnp.mean(x * x, axis=-1, keepdims=True)
        o_ref[...] = x * lax.rsqrt(ms + EPS) * g_ref[...]


def matmul_res_rmsnorm(a, w, res, g, *, tm, tn, name):
    M, K = a.shape
    N = w.shape[1]
    tm, tn = _tile(M, tm), _tile(N, tn)
    body = functools.partial(_mm_res_norm_body, n_tiles=N // tn, tn=tn)
    return pl.pallas_call(
        body,
        grid=(M // tm, N // tn),
        in_specs=[
            pl.BlockSpec((tm, K), lambda i, j: (i, 0)),
            pl.BlockSpec((K, tn), lambda i, j: (0, j)),
            pl.BlockSpec((tm, tn), lambda i, j: (i, j)),
            pl.BlockSpec((1, N), lambda i, j: (0, 0)),
        ],
        out_specs=pl.BlockSpec((tm, N), lambda i, j: (i, 0)),
        out_shape=jax.ShapeDtypeStruct((M, N), F32),
        compiler_params=_params(("parallel", "arbitrary"), 56),
        name=name,
    )(a, w, res, g.reshape(1, N).astype(F32))


def _in_proj_body(x_ref, g_ref, w_ref, *rest, with_extra, transposed):
    if with_extra:
        we_ref, o_ref, e_ref, h_s = rest
    else:
        o_ref, h_s = rest
    mm = _dot_nt if transposed else _dot

    @pl.when(pl.program_id(1) == 0)
    def _():
        x = x_ref[...]
        ms = jnp.mean(x * x, axis=-1, keepdims=True)
        h_s[...] = (x * lax.rsqrt(ms + EPS) * g_ref[...]).astype(BF16)
        if with_extra:
            e_ref[...] = mm(h_s[...], we_ref[...])

    o_ref[...] = mm(h_s[...], w_ref[...].astype(BF16)).astype(o_ref.dtype)


def in_projection(x, g, w, w_extra=None, *, n_cols, tm, tn, name, transposed=False):
    T, D = x.shape
    N = n_cols
    tm, tn = _tile(T, tm), _tile(N, tn)
    w_spec = pl.BlockSpec((tn, D), lambda i, j: (j, 0)) if transposed else pl.BlockSpec((D, tn), lambda i, j: (0, j))
    in_specs = [pl.BlockSpec((tm, D), lambda i, j: (i, 0)), pl.BlockSpec((1, D), lambda i, j: (0, 0)), w_spec]
    args = [x, g.reshape(1, D).astype(F32), w]
    out_specs = [pl.BlockSpec((tm, tn), lambda i, j: (i, j))]
    out_shape = [jax.ShapeDtypeStruct((T, N), BF16)]
    if w_extra is not None:
        ne = w_extra.shape[0] if transposed else w_extra.shape[1]
        in_specs.append(pl.BlockSpec(w_extra.shape, lambda i, j: (0, 0)))
        args.append(w_extra)
        out_specs.append(pl.BlockSpec((tm, ne), lambda i, j: (i, 0)))
        out_shape.append(jax.ShapeDtypeStruct((T, ne), F32))
    outs = pl.pallas_call(
        functools.partial(_in_proj_body, with_extra=w_extra is not None, transposed=transposed),
        grid=(T // tm, N // tn),
        in_specs=in_specs,
        out_specs=out_specs,
        out_shape=out_shape,
        scratch_shapes=[pltpu.VMEM((tm, D), BF16)],
        compiler_params=_params(("parallel", "arbitrary"), 56),
        name=name,
    )(*args)
    return outs if w_extra is not None else outs[0]


def _outproj_body(a1_ref, a2_ref, w1_ref, w2_ref, r_ref, g_ref, x_ref, h_ref):
    x = r_ref[...] + _dot(a1_ref[...], w1_ref[...]) + _dot(a2_ref[...], w2_ref[...])
    x_ref[...] = x
    ms = jnp.mean(x * x, axis=-1, keepdims=True)
    h_ref[...] = (x * lax.rsqrt(ms + EPS) * g_ref[...]).astype(h_ref.dtype)


def out_projection(a1, a2, w, res, g):
    T, K1 = a1.shape
    K2 = a2.shape[1]
    D = w.shape[1]
    assert K1 == K2 and w.shape[0] == K1 + K2
    tm = min(T, 512)
    return pl.pallas_call(
        _outproj_body,
        grid=(T // tm,),
        in_specs=[
            pl.BlockSpec((tm, K1), lambda i: (i, 0)),
            pl.BlockSpec((tm, K2), lambda i: (i, 0)),
            pl.BlockSpec((K1, D), lambda i: (0, 0)),
            pl.BlockSpec((K2, D), lambda i: (1, 0)),
            pl.BlockSpec((tm, D), lambda i: (i, 0)),
            pl.BlockSpec((1, D), lambda i: (0, 0)),
        ],
        out_specs=[pl.BlockSpec((tm, D), lambda i: (i, 0)), pl.BlockSpec((tm, D), lambda i: (i, 0))],
        out_shape=[jax.ShapeDtypeStruct((T, D), F32), jax.ShapeDtypeStruct((T, D), BF16)],
        compiler_params=_params(("parallel",), 56),
        name="out_projection",
    )(a1, a2, w, w, res, g.reshape(1, D).astype(F32))


HALO_ROWS = 16


def _outproj_conv_body(a1_ref, gb_ref, gc_ref, hin_ref, gch_ref, hinh_ref, cw_ref, w1_ref, w2_ref, r_ref, g_ref,
                       x_ref, h_ref, *, tiles_per_seq):
    first = pl.program_id(0) % tiles_per_seq == 0
    halo = gch_ref[...].astype(F32) * hinh_ref[...].astype(F32)
    halo = jnp.where(first, 0.0, halo)
    xe = jnp.concatenate([halo, gc_ref[...].astype(F32) * hin_ref[...].astype(F32)], axis=0)
    conv = _causal_conv3(xe, cw_ref[...])[HALO_ROWS:, :]
    a2 = (gb_ref[...].astype(F32) * conv).astype(BF16)
    x = r_ref[...] + _dot(a1_ref[...], w1_ref[...]) + _dot(a2, w2_ref[...])
    x_ref[...] = x
    ms = jnp.mean(x * x, axis=-1, keepdims=True)
    h_ref[...] = (x * lax.rsqrt(ms + EPS) * g_ref[...]).astype(h_ref.dtype)


def out_projection_conv(a1, p, conv_w, w, res, g, *, seq, col_block):
    T, K = a1.shape
    D = w.shape[1]
    assert w.shape[0] == 2 * K
    tm = min(seq, 512)
    assert seq % tm == 0 and tm % HALO_ROWS == 0
    hb = tm // HALO_ROWS

    def halo_rows(i):
        return jnp.maximum(i * hb - 1, 0)

    body = functools.partial(_outproj_conv_body, tiles_per_seq=seq // tm)
    return pl.pallas_call(
        body,
        grid=(T // tm,),
        in_specs=[
            pl.BlockSpec((tm, K), lambda i: (i, 0)),
            pl.BlockSpec((tm, K), lambda i: (i, col_block)),
            pl.BlockSpec((tm, K), lambda i: (i, col_block + 1)),
            pl.BlockSpec((tm, K), lambda i: (i, col_block + 2)),
            pl.BlockSpec((HALO_ROWS, K), lambda i: (halo_rows(i), col_block + 1)),
            pl.BlockSpec((HALO_ROWS, K), lambda i: (halo_rows(i), col_block + 2)),
            pl.BlockSpec((3, K), lambda i: (0, 0)),
            pl.BlockSpec((K, D), lambda i: (0, 0)),
            pl.BlockSpec((K, D), lambda i: (1, 0)),
            pl.BlockSpec((tm, D), lambda i: (i, 0)),
            pl.BlockSpec((1, D), lambda i: (0, 0)),
        ],
        out_specs=[pl.BlockSpec((tm, D), lambda i: (i, 0)), pl.BlockSpec((tm, D), lambda i: (i, 0))],
        out_shape=[jax.ShapeDtypeStruct((T, D), F32), jax.ShapeDtypeStruct((T, D), BF16)],
        compiler_params=_params(("parallel",), 56),
        name="out_projection_conv",
    )(a1, p, p, p, p, p, conv_w, w, w, res, g.reshape(1, D).astype(F32))


FFN_PIECE = 128


def _ffn_up_body(h_ref, wg_ref, wu_ref, cg_ref, cu_ref, o_ref, *, row_chunk):
    rows = h_ref.shape[0]
    pieces = [slice(c0, c0 + FFN_PIECE) for c0 in range(0, o_ref.shape[1], FFN_PIECE)]
    w = jnp.concatenate([ref[:, cs].astype(BF16) for cs in pieces for ref in (wg_ref, wu_ref)], axis=1)
    cw = jnp.concatenate([ref[:, cs] for cs in pieces for ref in (cg_ref, cu_ref)], axis=1)
    a = jnp.concatenate([_dot(h_ref[r:r + row_chunk, :], w) for r in range(0, rows, row_chunk)], axis=0)
    y = _causal_conv3(a, cw)
    for k, cs in enumerate(pieces):
        gate = y[:, 2 * k * FFN_PIECE:(2 * k + 1) * FFN_PIECE]
        up = y[:, (2 * k + 1) * FFN_PIECE:(2 * k + 2) * FFN_PIECE]
        o_ref[:, cs] = (gate * (1.0 / (1.0 + jnp.exp(-gate))) * up).astype(o_ref.dtype)


def ffn_up(h, w_up, conv_w, *, seq, tn):
    T, D = h.shape
    F = w_up.shape[1] // 2
    tn = _tile(F, tn)
    assert T % seq == 0 and tn % FFN_PIECE == 0
    nf = F // tn
    body = functools.partial(_ffn_up_body, row_chunk=_tile(seq, 512))
    return pl.pallas_call(
        body,
        grid=(T // seq, nf),
        in_specs=[
            pl.BlockSpec((seq, D), lambda b, j: (b, 0)),
            pl.BlockSpec((D, tn), lambda b, j: (0, j)),
            pl.BlockSpec((D, tn), lambda b, j: (0, nf + j)),
            pl.BlockSpec((3, tn), lambda b, j: (0, j)),
            pl.BlockSpec((3, tn), lambda b, j: (0, nf + j)),
        ],
        out_specs=pl.BlockSpec((seq, tn), lambda b, j: (b, j)),
        out_shape=jax.ShapeDtypeStruct((T, F), BF16),
        compiler_params=_params(("parallel", "arbitrary"), 56),
        name="ffn_up",
    )(h, w_up, w_up, conv_w, conv_w)


def _sb_attention_body(q_ref, k_ref, v_ref, o_ref, acc_ref, r_ref, *, n_heads, scale):
    blk = ATT_BLOCK
    i = pl.program_id(1)
    row = lax.broadcasted_iota(jnp.int32, (blk, blk), 0)
    col = lax.broadcasted_iota(jnp.int32, (blk, blk), 1)
    strict = col < row
    suffix = jnp.where(row >= col, 1.0, 0.0).astype(BF16)

    def block(j, diagonal):
        start = pl.multiple_of(j * blk, blk)
        heads = [slice(h * HEAD_DIM, (h + 1) * HEAD_DIM) for h in range(n_heads)]
        zs = [_dot_nt(q_ref[:, hs], k_ref[pl.ds(start, blk), hs]) * scale for hs in heads]
        css = []
        for z in zs:
            sp = _softplus(z)
            if diagonal:
                sp = jnp.where(strict, sp, 0.0)
            sp_hi = sp.astype(BF16)
            sp_lo = (sp - sp_hi.astype(F32)).astype(BF16)
            css.append(_dot(sp_hi, suffix) + _dot(sp_lo, suffix))
        r_min = None
        for h, hs in enumerate(heads):
            z, cs = zs[h], css[h]
            vb = v_ref[pl.ds(start, blk), hs]
            if diagonal:
                a = jnp.where(strict, jnp.exp(z - cs), 0.0)
                r = cs[:, 0:1]
                acc_ref[h] = _dot(a.astype(BF16), vb)
            else:
                r = r_ref[h]
                a = jnp.exp(z - (cs + r))
                r = r + cs[:, 0:1]
                acc_ref[h] += _dot(a.astype(BF16), vb)
            r_ref[h] = r
            r_min = r if r_min is None else jnp.minimum(r_min, r)
        return jnp.min(r_min)

    def cond(carry):
        return jnp.logical_and(carry[0] >= 0, carry[1] < DEAD_EXPONENT)

    def body(carry):
        return carry[0] - 1, block(carry[0], False)

    lax.while_loop(cond, body, (i - 1, block(i, True)))
    for h in range(n_heads):
        o_ref[:, h * HEAD_DIM:(h + 1) * HEAD_DIM] = acc_ref[h].astype(o_ref.dtype)


def sb_attention(p, *, seq, n_heads):
    T = p.shape[0]
    W = n_heads * HEAD_DIM
    blk = ATT_BLOCK
    assert seq % blk == 0
    nq = seq // blk
    body = functools.partial(_sb_attention_body, n_heads=n_heads, scale=HEAD_DIM ** -0.5)
    return pl.pallas_call(
        body,
        grid=(T // seq, nq),
        in_specs=[
            pl.BlockSpec((blk, W), lambda b, i: (b * nq + i, 0)),
            pl.BlockSpec((seq, W), lambda b, i: (b, 1)),
            pl.BlockSpec((seq, W), lambda b, i: (b, 2)),
        ],
        out_specs=pl.BlockSpec((blk, W), lambda b, i: (b * nq + i, 0)),
        out_shape=jax.ShapeDtypeStruct((T, W), BF16),
        scratch_shapes=[pltpu.VMEM((n_heads, blk, HEAD_DIM), F32), pltpu.VMEM((n_heads, blk, 1), F32)],
        compiler_params=_params(("parallel", "arbitrary"), 48),
        name="sb_attention",
    )(p, p, p)


def _forget_cumsum_body(f_ref, b_ref, c_ref):
    x = f_ref[0] + b_ref[...]
    lf = -_softplus(-x)
    S = lf.shape[1]
    lane = lax.broadcasted_iota(jnp.int32, lf.shape, 1)
    k = 1
    while k < S:
        lf = lf + jnp.where(lane >= k, pltpu.roll(lf, k, 1), 0.0)
        k *= 2
    c_ref[0] = lf


def forget_cumsum(f, bias):
    B, H, S = f.shape
    return pl.pallas_call(
        _forget_cumsum_body,
        grid=(B,),
        in_specs=[pl.BlockSpec((1, H, S), lambda b: (b, 0, 0)), pl.BlockSpec((H, 1), lambda b: (0, 0))],
        out_specs=pl.BlockSpec((1, H, S), lambda b: (b, 0, 0)),
        out_shape=jax.ShapeDtypeStruct((B, H, S), F32),
        compiler_params=_params(("parallel",), 16),
        name="forget_cumsum",
    )(f, bias.reshape(H, 1).astype(F32))


def _fox_attention_body(q_ref, k_ref, v_ref, c_ref, ccol_ref, o_ref, acc_ref, m_ref, kmax_ref, *, n_heads, scale):
    blk = ATT_BLOCK
    i = pl.program_id(1)
    row = lax.broadcasted_iota(jnp.int32, (blk, blk), 0)
    col = lax.broadcasted_iota(jnp.int32, (blk, blk), 1)
    causal = col <= row
    heads = [slice(h * HEAD_DIM, (h + 1) * HEAD_DIM) for h in range(n_heads)]
    ones_blk = jnp.ones((blk, HEAD_DIM), BF16)

    @pl.when(i == 0)
    def _():
        ones = jnp.ones((8, HEAD_DIM), BF16)
        for h, hs in enumerate(heads):
            k = k_ref[:, hs].astype(F32)
            n2 = _dot_nt(ones, (k * k).astype(BF16))[0:1, :]
            kmax_ref[h:h + 1, :] = jnp.broadcast_to(
                jnp.sqrt(jnp.max(n2, axis=-1, keepdims=True)) * NORM_SLACK, (1, HEAD_DIM))

    qk_bound = []
    for h, hs in enumerate(heads):
        q = q_ref[:, hs].astype(F32)
        qn2 = jnp.max(jnp.sum(q * q, axis=-1, keepdims=True), axis=0, keepdims=True)
        qk_bound.append(jnp.sqrt(qn2) * kmax_ref[h:h + 1, 0:1] * scale)

    def block(j, diagonal, fixed):
        start = pl.multiple_of(j * blk, blk)
        c_blk = c_ref[0, pl.ds(j, 1)][0]
        reach = None
        scores = [_dot_nt(q_ref[:, hs], k_ref[pl.ds(start, blk), hs]) for hs in heads]
        for h, hs in enumerate(heads):
            vb = jnp.concatenate([v_ref[pl.ds(start, blk), hs], ones_blk], axis=1)
            s = scores[h] * scale - c_blk[h:h + 1, :]
            if diagonal:
                s = jnp.where(causal, s, -jnp.inf)
            if fixed:
                c_t = ccol_ref[:, h:h + 1]
                pv = _dot(jnp.exp(s + (c_t - qk_bound[h])).astype(BF16), vb)
                acc_ref[h] = pv if diagonal else acc_ref[h] + pv
                r = (2.0 * qk_bound[h] - c_blk[h:h + 1, 0:1]) + c_t[0:1, :]
            else:
                if diagonal:
                    m = jnp.max(s, axis=-1, keepdims=True)
                    acc_ref[h] = _dot(jnp.exp(s - m).astype(BF16), vb)
                else:
                    m_old = m_ref[h]
                    m = jnp.maximum(m_old, jnp.max(s, axis=-1, keepdims=True))
                    acc_ref[h] = jnp.exp(m_old - m) * acc_ref[h] + _dot(jnp.exp(s - m).astype(BF16), vb)
                m_ref[h] = m
                r = (qk_bound[h] - c_blk[h:h + 1, 0:1]) - m
            reach = r if reach is None else jnp.maximum(reach, r)
        return jnp.max(reach)

    def cond(carry):
        return jnp.logical_and(carry[0] >= 0, carry[1] > -DEAD_EXPONENT)

    def sweep(fixed):
        def body(carry):
            return carry[0] - 1, block(carry[0], False, fixed)

        lax.while_loop(cond, body, (i - 1, block(i, True, fixed)))

    small_logits = 2.0 * jnp.max(jnp.concatenate(qk_bound, axis=1)) <= FIXED_CEILING_LIMIT

    @pl.when(small_logits)
    def _():
        sweep(True)

    @pl.when(jnp.logical_not(small_logits))
    def _():
        sweep(False)

    for h, hs in enumerate(heads):
        o_ref[:, hs] = (acc_ref[h, :, 0:HEAD_DIM] / acc_ref[h, :, HEAD_DIM:]).astype(o_ref.dtype)


def fox_attention(p, c, *, seq, n_heads, col_block):
    T = p.shape[0]
    B = T // seq
    W = n_heads * HEAD_DIM
    blk = ATT_BLOCK
    nq = seq // blk
    c_blocks = c.reshape(B, n_heads, nq, blk).transpose(0, 2, 1, 3)
    c_col = c.transpose(0, 2, 1).reshape(T, n_heads)
    body = functools.partial(_fox_attention_body, n_heads=n_heads, scale=HEAD_DIM ** -0.5)
    return pl.pallas_call(
        body,
        grid=(B, nq),
        in_specs=[
            pl.BlockSpec((blk, W), lambda b, i: (b * nq + i, col_block)),
            pl.BlockSpec((seq, W), lambda b, i: (b, col_block + 1)),
            pl.BlockSpec((seq, W), lambda b, i: (b, col_block + 2)),
            pl.BlockSpec((1, nq, n_heads, blk), lambda b, i: (b, 0, 0, 0)),
            pl.BlockSpec((blk, n_heads), lambda b, i: (b * nq + i, 0)),
        ],
        out_specs=pl.BlockSpec((blk, W), lambda b, i: (b * nq + i, 0)),
        out_shape=jax.ShapeDtypeStruct((T, W), BF16),
        scratch_shapes=[
            pltpu.VMEM((n_heads, blk, 2 * HEAD_DIM), F32),
            pltpu.VMEM((n_heads, blk, 1), F32),
            pltpu.VMEM((n_heads, HEAD_DIM), F32),
        ],
        compiler_params=_params(("arbitrary", "arbitrary"), 48),
        name="fox_attention",
    )(p, p, p, c_blocks, c_col)


def _spatial_gate_body(u_ref, v_ref, w_ref, bt_ref, g_ref, o_ref, *, n_groups):
    tm = u_ref.shape[0]
    v = _gelu_tanh(v_ref[...].astype(F32))
    mu = jnp.mean(v, axis=-1, keepdims=True)
    vc = v - mu
    var = jnp.mean(vc * vc, axis=-1, keepdims=True)
    vn = (vc * lax.rsqrt(var + EPS) * g_ref[...]).astype(BF16)
    row = lax.broadcasted_iota(jnp.int32, (CHUNK, CHUNK), 0)
    col = lax.broadcasted_iota(jnp.int32, (CHUNK, CHUNK), 1)
    for g in range(n_groups):
        gs = slice(g * SG_GROUP_DIM, (g + 1) * SG_GROUP_DIM)
        w = jnp.where(col <= row, w_ref[g], 0.0).astype(BF16)
        b = bt_ref[:, g:g + 1]
        for c in range(tm // CHUNK):
            cs = slice(c * CHUNK, (c + 1) * CHUNK)
            mixed = _dot(w, vn[cs, gs]) + b
            o_ref[cs, gs] = (_gelu_tanh(u_ref[cs, gs].astype(F32)) * mixed).astype(o_ref.dtype)


def spatial_gate(p, sg_w, sg_b, sg_g, *, width):
    T = p.shape[0]
    G = width // SG_GROUP_DIM
    tm = min(T, 512)
    assert tm % CHUNK == 0
    body = functools.partial(_spatial_gate_body, n_groups=G)
    return pl.pallas_call(
        body,
        grid=(T // tm,),
        in_specs=[
            pl.BlockSpec((tm, width), lambda i: (i, 0)),
            pl.BlockSpec((tm, width), lambda i: (i, 1)),
            pl.BlockSpec((G, CHUNK, CHUNK), lambda i: (0, 0, 0)),
            pl.BlockSpec((CHUNK, G), lambda i: (0, 0)),
            pl.BlockSpec((1, width), lambda i: (0, 0)),
        ],
        out_specs=pl.BlockSpec((tm, width), lambda i: (i, 0)),
        out_shape=jax.ShapeDtypeStruct((T, width), BF16),
        compiler_params=_params(("parallel",), 32),
        name="spatial_gate",
    )(p, p, sg_w, sg_b.T, sg_g.reshape(1, width).astype(F32))


def _conv_ffn(x, h, w_up, conv_w, w_down, seq, final_g=None):
    g = ffn_up(h, w_up, conv_w, seq=seq, tn=512)
    if final_g is None:
        return matmul_res(g, w_down.astype(BF16), x, tm=1024, tn=512, name="ffn_down")
    return matmul_res_rmsnorm(g, w_down.astype(BF16), x, final_g, tm=512, tn=1024, name="ffn_down_norm")


def kernel(x, l0_mix_norm_g, l0_w_in, l0_sc_conv_w, l0_w_out, l0_ffn_norm_g, l0_ffn_up, l0_ffn_conv_w, l0_ffn_down, l1_mix_norm_g, l1_w_in, l1_fox_b_f, l1_sg_w, l1_sg_b, l1_sg_norm_g, l1_w_out, l1_ffn_norm_g, l1_ffn_up, l1_ffn_conv_w, l1_ffn_down, final_norm_g):
    B, S, D = x.shape
    T = B * S
    half = D // 2
    n_heads = half // HEAD_DIM
    x = x.reshape(T, D)

    p = in_projection(x, l0_mix_norm_g, l0_w_in, n_cols=6 * half, tm=1024, tn=1024, name="in_projection0")
    a_out = sb_attention(p, seq=S, n_heads=n_heads)
    x, h = out_projection_conv(a_out, p, l0_sc_conv_w, l0_w_out.astype(BF16), x, l0_ffn_norm_g,
                               seq=S, col_block=3)
    x = _conv_ffn(x, h, l0_ffn_up, l0_ffn_conv_w, l0_ffn_down, S)

    n_main = 5 * half
    w_t = l1_w_in.T
    w_f = jnp.pad(w_t[n_main:], ((0, HEAD_DIM - n_heads), (0, 0))).astype(BF16)
    p, f = in_projection(x, l1_mix_norm_g, w_t[:n_main].astype(BF16), w_f, n_cols=n_main,
                         tm=1024, tn=1024, name="in_projection1", transposed=True)
    f = f[:, :n_heads].reshape(B, S, n_heads).transpose(0, 2, 1)
    c = forget_cumsum(f, l1_fox_b_f)
    c_out = spatial_gate(p, l1_sg_w, l1_sg_b, l1_sg_norm_g, width=half)
    d_out = fox_attention(p, c, seq=S, n_heads=n_heads, col_block=2)
    x, h = out_projection(c_out, d_out, l1_w_out.astype(BF16), x, l1_ffn_norm_g)
    return _conv_ffn(x, h, l1_ffn_up, l1_ffn_conv_w, l1_ffn_down, S, final_norm_g).reshape(B, S, D)
```
